```python
import math
import jax, jax.numpy as jnp
from jax import lax
import numpy as np

D_MODEL = 1024
BATCH = 8
SEQ = 8192
DEPTH = 2
DEC_BATCH = 16
DEC_SEQ = 2048
PAST_LEN = 128

CONV_WIDTH = D_MODEL
CONV_KERNEL = 31
RET_HEADS = 4
RET_DK = 256
RET_DV = 256
RET_QK = RET_HEADS * RET_DK
RET_V = RET_HEADS * RET_DV
CHUNK = 128
ROPE_BASE = 10000.0
IN_COLS = 2 * CONV_WIDTH + 2 * RET_QK + 2 * RET_V
PEER_HEADS = 8
PEER_NK = 128
PEER_NE = PEER_NK * PEER_NK
PEER_DQ = 256
PEER_TOPK = 16
PEER_BLOCK = 128
EPS = 1e-6

kernel_name = "hybrid_conformer_retention_peer_encoder"


def _rmsnorm(x, g):
    xf = x.astype(jnp.float32)
    y = xf * lax.rsqrt(jnp.mean(xf * xf, axis=-1, keepdims=True) + EPS)
    return (y * g.astype(jnp.float32)).astype(x.dtype)


def _layernorm(x, g, b):
    xf = x.astype(jnp.float32)
    mu = jnp.mean(xf, axis=-1, keepdims=True)
    var = jnp.mean(jnp.square(xf - mu), axis=-1, keepdims=True)
    y = (xf - mu) * lax.rsqrt(var + EPS)
    return (y * g.astype(jnp.float32) + b.astype(jnp.float32)).astype(x.dtype)


def _rope(t):
    L = t.shape[1]
    half = t.shape[-1] // 2
    freqs = 1.0 / (ROPE_BASE ** (jnp.arange(half, dtype=jnp.float32) / half))
    ang = jnp.arange(L, dtype=jnp.float32)[:, None] * freqs[None, :]
    cos = jnp.cos(ang)[None, :, None, :]
    sin = jnp.sin(ang)[None, :, None, :]
    t1, t2 = t[..., :half], t[..., half:]
    return jnp.concatenate([t1 * cos - t2 * sin, t1 * sin + t2 * cos], axis=-1)


def _retention_dir(q, k, v, log_gamma, strict):
    B, L, H, DK = q.shape
    DV = v.shape[-1]
    N = L // CHUNK
    lg = log_gamma.astype(jnp.float32)
    qc = q.reshape(B, N, CHUNK, H, DK)
    kc = k.reshape(B, N, CHUNK, H, DK)
    vc = v.reshape(B, N, CHUNK, H, DV)
    pos = jnp.arange(CHUNK, dtype=jnp.float32)
    diff = pos[:, None] - pos[None, :]
    mask = (diff > 0) if strict else (diff >= 0)
    decay = jnp.where(mask[None], jnp.exp(jnp.where(mask, diff, 0.0)[None] * lg[:, None, None]), 0.0)
    scores = jnp.einsum('bnihd,bnjhd->bnhij', qc, kc) * decay
    intra = jnp.einsum('bnhij,bnjhe->bnihe', scores, vc)
    q_dec = qc * jnp.exp((pos + 1.0)[:, None] * lg[None, :])[..., None]
    k_dec = kc * jnp.exp((CHUNK - 1.0 - pos)[:, None] * lg[None, :])[..., None]
    chunk_decay = jnp.exp(CHUNK * lg)

    def step(S, xs):
        qd, kd, vv = xs
        cross = jnp.einsum('bihd,bhde->bihe', qd, S)
        S = S * chunk_decay[None, :, None, None] + jnp.einsum('bjhd,bjhe->bhde', kd, vv)
        return S, cross

    S0 = jnp.zeros((B, H, DK, DV), jnp.float32)
    _, cross = lax.scan(step, S0, (jnp.moveaxis(q_dec, 1, 0), jnp.moveaxis(k_dec, 1, 0), jnp.moveaxis(vc, 1, 0)))
    cross = jnp.moveaxis(cross, 0, 1)
    return (intra + cross).reshape(B, L, H, DV)


def _mixer(xn, w_in, w_gate, conv_w, conv_b, conv_ln_g, conv_ln_b, w_conv_out,
           log_gamma_fwd, log_gamma_bwd, w_ret_out, w_o):
    B, L, _ = xn.shape
    z = xn @ w_in
    c0 = CONV_WIDTH
    c1 = 2 * CONV_WIDTH
    c2 = c1 + RET_QK
    c3 = c2 + RET_QK
    c4 = c3 + RET_V
    glu_a, glu_b, q, k, v, g = (z[..., :c0], z[..., c0:c1], z[..., c1:c2],
                                z[..., c2:c3], z[..., c3:c4], z[..., c4:])
    u = glu_a * jax.nn.sigmoid(glu_b)
    u = lax.conv_general_dilated(u, conv_w[:, None, :], window_strides=(1,),
                                 padding=[(CONV_KERNEL // 2, CONV_KERNEL // 2)],
                                 dimension_numbers=('NWC', 'WIO', 'NWC'),
                                 feature_group_count=CONV_WIDTH) + conv_b
    u = jax.nn.silu(_layernorm(u, conv_ln_g, conv_ln_b))
    y_conv = u @ w_conv_out
    qh = _rope(q.reshape(B, L, RET_HEADS, RET_DK).astype(jnp.float32))
    kh = _rope(k.reshape(B, L, RET_HEADS, RET_DK).astype(jnp.float32)) * (RET_DK ** -0.5)
    vh = v.reshape(B, L, RET_HEADS, RET_DV).astype(jnp.float32)
    rev = lambda t: jnp.flip(t, axis=1)
    o = (_retention_dir(qh, kh, vh, log_gamma_fwd, False)
         + rev(_retention_dir(rev(qh), rev(kh), rev(vh), log_gamma_bwd, True)))
    o = o * lax.rsqrt(jnp.mean(o * o, axis=-1, keepdims=True) + EPS)
    o = o.reshape(B, L, RET_V).astype(xn.dtype)
    y_ret = (jax.nn.silu(g) * o) @ w_ret_out
    gates = jax.nn.sigmoid(xn @ w_gate)
    merged = gates[..., :D_MODEL] * y_conv + gates[..., D_MODEL:] * y_ret
    return merged @ w_o


def _peer(x, wq, subkeys, u_tab, v_tab):
    B, L, D = x.shape
    xb = x.reshape(-1, PEER_BLOCK, D)

    def blk(xt):
        q = (xt @ wq).reshape(PEER_BLOCK, PEER_HEADS, 2, PEER_DQ // 2)
        s = jnp.einsum('thpd,hpkd->thpk', q, subkeys).astype(jnp.float32)
        sv, si = lax.top_k(s, PEER_TOPK)
        cand = (sv[:, :, 0, :, None] + sv[:, :, 1, None, :]).reshape(PEER_BLOCK, PEER_HEADS, PEER_TOPK * PEER_TOPK)
        sc, ci = lax.top_k(cand, PEER_TOPK)
        i1 = jnp.take_along_axis(si[:, :, 0], ci // PEER_TOPK, axis=-1)
        i2 = jnp.take_along_axis(si[:, :, 1], ci % PEER_TOPK, axis=-1)
        e = i1 * PEER_NK + i2
        gate = jax.nn.softmax(sc, axis=-1).astype(xt.dtype)
        hid = jax.nn.gelu(jnp.einsum('td,thkd->thk', xt, u_tab[e]), approximate=False)
        return jnp.einsum('thk,thkd->td', gate * hid, v_tab[e])

    return lax.map(blk, xb).reshape(B, L, D)


def _trunk(x, norm_mix_g, w_in, w_gate, conv_w, conv_b, conv_ln_g, conv_ln_b, w_conv_out,
           log_gamma_fwd, log_gamma_bwd, w_ret_out, w_o, norm_ffn_g, peer_wq, peer_subkeys,
           peer_u, peer_v, final_norm_g):
    for l in range(DEPTH):
        x = x + _mixer(_rmsnorm(x, norm_mix_g[l]), w_in[l], w_gate[l], conv_w[l], conv_b[l],
                       conv_ln_g[l], conv_ln_b[l], w_conv_out[l], log_gamma_fwd[l],
                       log_gamma_bwd[l], w_ret_out[l], w_o[l])
        x = x + _peer(_rmsnorm(x, norm_ffn_g[l]), peer_wq[l], peer_subkeys[l], peer_u[l], peer_v[l])
    return _rmsnorm(x, final_norm_g)


def setup_inputs(seed: int = 0) -> dict:
    key = jax.random.key(seed)
    ks = jax.random.split(key, 24)
    f32 = jnp.float32
    nrm = lambda k, shape, scale: jax.random.normal(k, shape, f32) * scale
    base_exp = 5.0 + jnp.arange(RET_HEADS, dtype=f32)
    lg_f = jnp.log1p(-jnp.exp2(-(base_exp[None] + 0.1 * jax.random.normal(ks[10], (DEPTH, RET_HEADS), f32))))
    lg_b = jnp.log1p(-jnp.exp2(-(base_exp[None] + 0.1 * jax.random.normal(ks[11], (DEPTH, RET_HEADS), f32))))
    return {
        "x_prompt": nrm(ks[0], (BATCH, SEQ, D_MODEL), 1.0),
        "x_sample": nrm(ks[1], (DEC_BATCH, DEC_SEQ, D_MODEL), 1.0),
        "norm_mix_g": 1.0 + nrm(ks[2], (DEPTH, D_MODEL), 0.02),
        "w_in": nrm(ks[3], (DEPTH, D_MODEL, IN_COLS), D_MODEL ** -0.5),
        "w_gate": nrm(ks[4], (DEPTH, D_MODEL, 2 * D_MODEL), D_MODEL ** -0.5),
        "conv_w": nrm(ks[5], (DEPTH, CONV_KERNEL, CONV_WIDTH), CONV_KERNEL ** -0.5),
        "conv_b": nrm(ks[6], (DEPTH, CONV_WIDTH), 0.02),
        "conv_ln_g": 1.0 + nrm(ks[7], (DEPTH, CONV_WIDTH), 0.02),
        "conv_ln_b": nrm(ks[8], (DEPTH, CONV_WIDTH), 0.02),
        "w_conv_out": nrm(ks[9], (DEPTH, CONV_WIDTH, D_MODEL), CONV_WIDTH ** -0.5),
        "log_gamma_fwd": lg_f,
        "log_gamma_bwd": lg_b,
        "w_ret_out": nrm(ks[12], (DEPTH, RET_V, D_MODEL), RET_V ** -0.5),
        "w_o": nrm(ks[13], (DEPTH, D_MODEL, D_MODEL), D_MODEL ** -0.5),
        "norm_ffn_g": 1.0 + nrm(ks[14], (DEPTH, D_MODEL), 0.02),
        "peer_wq": nrm(ks[15], (DEPTH, D_MODEL, PEER_HEADS * PEER_DQ), D_MODEL ** -0.5),
        "peer_subkeys": nrm(ks[16], (DEPTH, PEER_HEADS, 2, PEER_NK, PEER_DQ // 2), (PEER_DQ // 2) ** -0.5),
        "peer_u": nrm(ks[17], (DEPTH, PEER_NE, D_MODEL), D_MODEL ** -0.5),
        "peer_v": nrm(ks[18], (DEPTH, PEER_NE, D_MODEL), 0.5 * PEER_HEADS ** -0.5),
        "final_norm_g": 1.0 + nrm(ks[19], (D_MODEL,), 0.02),
    }


def reference(x_prompt, x_sample, norm_mix_g, w_in, w_gate, conv_w, conv_b, conv_ln_g, conv_ln_b,
              w_conv_out, log_gamma_fwd, log_gamma_bwd, w_ret_out, w_o, norm_ffn_g, peer_wq,
              peer_subkeys, peer_u, peer_v, final_norm_g):
    y_prompt = _trunk(x_prompt, norm_mix_g, w_in, w_gate, conv_w, conv_b, conv_ln_g, conv_ln_b,
                      w_conv_out, log_gamma_fwd, log_gamma_bwd, w_ret_out, w_o, norm_ffn_g, peer_wq,
                      peer_subkeys, peer_u, peer_v, final_norm_g)
    y_sample = _trunk(x_sample, norm_mix_g, w_in, w_gate, conv_w, conv_b, conv_ln_g, conv_ln_b,
                      w_conv_out, log_gamma_fwd, log_gamma_bwd, w_ret_out, w_o, norm_ffn_g, peer_wq,
                      peer_subkeys, peer_u, peer_v, final_norm_g)
    return (y_prompt, y_sample)
```

```python
import functools
import math

import jax
import jax.numpy as jnp
from jax import lax
from jax.experimental import pallas as pl
from jax.experimental.pallas import tpu as pltpu

F32 = jnp.float32
BF16 = jnp.bfloat16

EPS = 1e-6
RET_HEADS = 4
CHUNK = 128
ROPE_BASE = 10000.0
CONV_KERNEL = 31
PEER_TOPK = 16

LANES = 128
BF16_ROWS = 16
CONV_HALO = 16
VMEM_LIMIT_BYTES = 56 * 1024 * 1024


def _const_spec(shape):
    nd = len(shape)
    return pl.BlockSpec(shape, lambda *_: (0,) * nd, pipeline_mode=pl.Buffered(1))


def _params(*sem):
    return pltpu.CompilerParams(dimension_semantics=sem, vmem_limit_bytes=VMEM_LIMIT_BYTES)


def _rmsnorm_rows(x, g):
    return x * lax.rsqrt(jnp.mean(x * x, axis=-1, keepdims=True) + EPS) * g


def _inproj_body(x_ref, g_ref, w_ref, cos_ref, sin_ref, u_ref, q_ref, k_ref, v_ref, gs_ref, *, heads):
    x = x_ref[...]
    d = x.shape[-1]
    xn = _rmsnorm_rows(x, g_ref[...]).astype(BF16)

    def seg(n):
        return jnp.dot(xn, w_ref[:, n * d:(n + 1) * d], preferred_element_type=F32)

    u_ref[...] = (seg(0) * jax.nn.sigmoid(seg(1))).astype(BF16)
    cos = cos_ref[...]
    sin = sin_ref[...]
    dk = d // heads
    half = dk // 2

    def rope_store(z, o_ref, scale):
        for h in range(heads):
            lo = slice(h * dk, h * dk + half)
            hi = slice(h * dk + half, (h + 1) * dk)
            t1 = z[:, lo]
            t2 = z[:, hi]
            o_ref[:, lo] = ((t1 * cos - t2 * sin) * scale).astype(BF16)
            o_ref[:, hi] = ((t1 * sin + t2 * cos) * scale).astype(BF16)

    rope_store(seg(2), q_ref, 1.0)
    rope_store(seg(3), k_ref, float(dk) ** -0.5)
    v_ref[...] = seg(4).astype(BF16)
    gs_ref[...] = jax.nn.silu(seg(5)).astype(BF16)


def _inproj(xf, g, w_in, cos, sin, *, seq_len, ts):
    t, d = xf.shape
    half = cos.shape[-1]
    n_seq_tiles = seq_len // ts
    tok = pl.BlockSpec((ts, d), lambda i: (i, 0))
    pos = pl.BlockSpec((ts, half), lambda i: (i % n_seq_tiles, 0))
    out = jax.ShapeDtypeStruct((t, d), BF16)
    return pl.pallas_call(
        functools.partial(_inproj_body, heads=RET_HEADS),
        grid=(t // ts,),
        in_specs=[tok, _const_spec((1, d)), _const_spec(w_in.shape), pos, pos],
        out_specs=[tok] * 5,
        out_shape=[out] * 5,
        compiler_params=_params("parallel"),
        name="inproj",
    )(xf, g, w_in, cos, sin)


def _retention_body(lgf_ref, lgb_ref, q_ref, k_ref, v_ref, *rest, heads, chunk, final):
    if final:
        cb_ref, gs_ref, o_ref, s_ref, qd_ref, kd_ref, cd_ref, dm_ref = rest
    else:
        o_ref, s_ref, qd_ref, kd_ref, cd_ref = rest
        dm_ref = None
    d = q_ref.shape[-1]
    dk = d // heads
    c = chunk
    n_chunks = q_ref.shape[1] // c

    @pl.when(pl.program_id(1) == 0)
    def _():
        s_ref[...] = jnp.zeros(s_ref.shape, F32)
        row = lax.broadcasted_iota(jnp.int32, (c, dk), 0).astype(F32)
        ri = lax.broadcasted_iota(jnp.int32, (c, c), 0)
        ci = lax.broadcasted_iota(jnp.int32, (c, c), 1)
        for h in range(heads):
            cols = slice(h * dk, (h + 1) * dk)
            if final:
                lg = lgf_ref[h]
                qd_ref[:, cols] = jnp.exp((row + 1.0) * lg)
                kd_ref[:, cols] = jnp.exp((c - 1.0 - row) * lg)
                lower = ri >= ci
                diff = jnp.where(lower, ri - ci, ci - ri).astype(F32)
                dm_ref[h] = jnp.exp(diff * jnp.where(lower, lg, lgb_ref[h]))
            else:
                lg = lgb_ref[h]
                qd_ref[:, cols] = jnp.exp((float(c) - row) * lg)
                kd_ref[:, cols] = jnp.exp(row * lg)
            cd_ref[:, cols] = jnp.exp(jnp.full((1, dk), float(c), F32) * lg)

    order = range(n_chunks) if final else range(n_chunks - 1, -1, -1)
    for n in order:
        rows = slice(n * c, (n + 1) * c)
        for h in range(heads):
            cols = slice(h * dk, (h + 1) * dk)
            qc = q_ref[0, rows, cols]
            kc = k_ref[0, rows, cols]
            vc = v_ref[0, rows, cols]
            state = s_ref[h]
            qdec = (qc.astype(F32) * qd_ref[:, cols]).astype(BF16)
            kdec = (kc.astype(F32) * kd_ref[:, cols]).astype(BF16)
            cross = jnp.dot(qdec, state.astype(BF16), preferred_element_type=F32)
            s_ref[h] = state * cd_ref[:, cols] + lax.dot_general(
                kdec, vc, (((0,), (0,)), ((), ())), preferred_element_type=F32)
            if final:
                scores = lax.dot_general(qc, kc, (((1,), (1,)), ((), ())), preferred_element_type=F32)
                intra = jnp.dot((scores * dm_ref[h]).astype(BF16), vc, preferred_element_type=F32)
                o = intra + cross + cb_ref[0, rows, cols].astype(F32)
                o = o * lax.rsqrt(jnp.mean(o * o, axis=-1, keepdims=True) + EPS)
                o_ref[0, rows, cols] = (gs_ref[0, rows, cols].astype(F32) * o).astype(BF16)
            else:
                o_ref[0, rows, cols] = cross.astype(BF16)


def _retention(lgf, lgb, q, k, v, cb=None, gs=None, *, ts):
    b, l, d = q.shape
    final = cb is not None
    nb = l // ts
    dk = d // RET_HEADS
    if final:
        blk = pl.BlockSpec((1, ts, d), lambda bi, j: (bi, j, 0))
    else:
        blk = pl.BlockSpec((1, ts, d), lambda bi, j: (bi, nb - 1 - j, 0))
    smem = pl.BlockSpec(memory_space=pltpu.SMEM)
    scratch = [pltpu.VMEM((RET_HEADS, dk, dk), F32),
               pltpu.VMEM((CHUNK, d), F32),
               pltpu.VMEM((CHUNK, d), F32),
               pltpu.VMEM((1, d), F32)]
    args = [lgf, lgb, q, k, v]
    in_specs = [smem, smem, blk, blk, blk]
    if final:
        scratch.append(pltpu.VMEM((RET_HEADS, CHUNK, CHUNK), F32))
        args += [cb, gs]
        in_specs += [blk, blk]
    return pl.pallas_call(
        functools.partial(_retention_body, heads=RET_HEADS, chunk=CHUNK, final=final),
        grid=(b, nb),
        in_specs=in_specs,
        out_specs=blk,
        out_shape=jax.ShapeDtypeStruct((b, l, d), BF16),
        scratch_shapes=scratch,
        compiler_params=_params("arbitrary", "arbitrary"),
        name="retention_fwd" if final else "retention_bwd",
    )(*args)


CONV_ROWS = 64
CONV_COLS = 256


def _post_body(x_ref, up_ref, uc_ref, un_ref, r_ref, gmix_ref, cw_ref, cbias_ref, lng_ref, lnb_ref,
               wco_ref, wro_ref, wg_ref, wo_ref, o_ref, win_ref, y_ref, *, n_tiles):
    i = pl.program_id(1)
    ts, d = uc_ref.shape[1], uc_ref.shape[2]
    halo = up_ref.shape[1]
    n_win = ts // CONV_ROWS
    taps = cw_ref.shape[0]
    pad = taps // 2

    prev = jnp.where(i > 0, up_ref[0].astype(F32), 0.0)
    nxt = jnp.where(i < n_tiles - 1, un_ref[0].astype(F32), 0.0)
    for w in range(n_win):
        lo = w * CONV_ROWS
        win_ref[w, halo:halo + CONV_ROWS, :] = uc_ref[0, lo:lo + CONV_ROWS, :].astype(F32)
        if w == 0:
            win_ref[w, 0:halo, :] = prev
        else:
            win_ref[w, 0:halo, :] = uc_ref[0, lo - halo:lo, :].astype(F32)
        if w == n_win - 1:
            win_ref[w, halo + CONV_ROWS:, :] = nxt
        else:
            win_ref[w, halo + CONV_ROWS:, :] = uc_ref[0, lo + CONV_ROWS:lo + CONV_ROWS + halo, :].astype(F32)

    def conv_rows(w, carry):
        for cb in range(d // CONV_COLS):
            cols = slice(cb * CONV_COLS, (cb + 1) * CONV_COLS)
            acc = jnp.broadcast_to(cbias_ref[:, cols], (CONV_ROWS, CONV_COLS))
            for t in range(taps):
                off = halo - pad + t
                acc = acc + cw_ref[t:t + 1, cols] * win_ref[w, off:off + CONV_ROWS, cols]
            y_ref[pl.ds(pl.multiple_of(w * CONV_ROWS, CONV_ROWS), CONV_ROWS), cols] = acc
        return carry

    lax.fori_loop(0, n_win, conv_rows, 0)

    y = y_ref[...]
    mu = jnp.mean(y, axis=-1, keepdims=True)
    yc = y - mu
    var = jnp.mean(yc * yc, axis=-1, keepdims=True)
    yn = yc * lax.rsqrt(var + EPS) * lng_ref[...] + lnb_ref[...]
    y_conv = jnp.dot(jax.nn.silu(yn).astype(BF16), wco_ref[...], preferred_element_type=F32)
    y_ret = jnp.dot(r_ref[0], wro_ref[...], preferred_element_type=F32)
    x = x_ref[0]
    xn = _rmsnorm_rows(x, gmix_ref[...]).astype(BF16)
    gates = jax.nn.sigmoid(jnp.dot(xn, wg_ref[...], preferred_element_type=F32))
    merged = gates[:, :d] * y_conv + gates[:, d:] * y_ret
    o_ref[0] = x + jnp.dot(merged.astype(BF16), wo_ref[...], preferred_element_type=F32)


def _post(x, u, r, gmix, conv_w, conv_b, ln_g, ln_b, w_conv_out, w_ret_out, w_gate, w_o, *, ts):
    b, l, d = x.shape
    n_tiles = l // ts
    hpt = ts // CONV_HALO
    n_halo = l // CONV_HALO
    tile = pl.BlockSpec((1, ts, d), lambda bi, i: (bi, i, 0))
    prev = pl.BlockSpec((1, CONV_HALO, d), lambda bi, i: (bi, jnp.maximum(i * hpt - 1, 0), 0))
    nxt = pl.BlockSpec((1, CONV_HALO, d), lambda bi, i: (bi, jnp.minimum((i + 1) * hpt, n_halo - 1), 0))
    vec = _const_spec((1, d))
    return pl.pallas_call(
        functools.partial(_post_body, n_tiles=n_tiles),
        grid=(b, n_tiles),
        in_specs=[tile, prev, tile, nxt, tile, vec, _const_spec(conv_w.shape), vec, vec, vec,
                  _const_spec(w_conv_out.shape), _const_spec(w_ret_out.shape),
                  _const_spec(w_gate.shape), _const_spec(w_o.shape)],
        out_specs=tile,
        out_shape=jax.ShapeDtypeStruct((b, l, d), F32),
        scratch_shapes=[pltpu.VMEM((ts // CONV_ROWS, CONV_ROWS + 2 * CONV_HALO, d), F32),
                        pltpu.VMEM((ts, d), F32)],
        compiler_params=_params("parallel", "parallel"),
        name="post",
    )(x, u, u, u, r, gmix, conv_w, conv_b, ln_g, ln_b, w_conv_out, w_ret_out, w_gate, w_o)


def _peer_body(x_ref, g_ref, wqt_ref, sk_ref, u_ref, vt_ref, *rest, heads, topk, final):
    if final:
        fg_ref = rest[0]
        rest = rest[1:]
    (o_ref, xnt_ref, st_ref, rank_ref, sv_ref, c_ref, iz_ref, cnt1_ref, ea1_ref, rk2_ref, eb2_ref,
     hid_ref, pt_ref, acc_ref) = rest
    e = pl.program_id(1)
    n_e = pl.num_programs(1)
    tb, d = x_ref.shape
    n_groups, nk, dq = sk_ref.shape
    nt = tb // LANES
    et = u_ref.shape[0]
    neg_inf = float("-inf")

    @pl.when(e == 0)
    def _routing():
        xn = _rmsnorm_rows(x_ref[...], g_ref[...])
        xnt = xn.T.astype(BF16)
        xnt_ref[...] = xnt
        qt = jnp.dot(wqt_ref[...], xnt, preferred_element_type=F32).astype(BF16)
        for g in range(n_groups):
            s = jnp.dot(sk_ref[g], qt[g * dq:(g + 1) * dq, :], preferred_element_type=F32)
            for tt in range(nt):
                st_ref[g, tt] = s[:, tt * LANES:(tt + 1) * LANES]
        acc_ref[...] = jnp.zeros(acc_ref.shape, F32)

        def extract(n, carry):
            g = n // nt
            tt = n % nt
            h = g // 2
            p = g % 2
            s = st_ref[g, tt]
            rank = jnp.full((nk, LANES), float(topk), F32)
            for r in range(topk):
                m = jnp.max(s, axis=0, keepdims=True)
                eq = s == m
                rank = jnp.where(eq, float(r), rank)
                s = jnp.where(eq, neg_inf, s)
                sv_ref[tt, p, r, pl.ds(h, 1), :] = m
            rank_ref[g, tt] = rank
            return carry

        lax.fori_loop(0, n_groups * nt, extract, 0)

        pairs = [(a, b) for a in range(topk) for b in range(topk) if (a + 1) * (b + 1) <= topk]

        def select(tt, carry):
            sv1 = [sv_ref[tt, 0, r] for r in range(topk)]
            sv2 = [sv_ref[tt, 1, r] for r in range(topk)]
            cand = [sv1[a] + sv2[b] for a, b in pairs]
            work = list(cand)
            tau = None
            for it in range(topk):
                tau = functools.reduce(jnp.maximum, work)
                if it < topk - 1:
                    work = [jnp.where(w == tau, neg_inf, w) for w in work]
            ea = [jnp.exp(v - sv1[0]) for v in sv1]
            eb = [jnp.exp(v - sv2[0]) for v in sv2]
            cnt = [jnp.zeros((heads, LANES), F32) for _ in range(topk)]
            z = jnp.zeros((heads, LANES), F32)
            for (a, b), cv in zip(pairs, cand):
                sel = cv >= tau
                cnt[a] = cnt[a] + jnp.where(sel, 1.0, 0.0)
                z = z + jnp.where(sel, ea[a] * eb[b], 0.0)
            for a in range(topk):
                c_ref[tt, a] = cnt[a]
            iz_ref[tt] = 1.0 / z
            return carry

        lax.fori_loop(0, nt, select, 0)

        def tables(n, carry):
            tt = n // heads
            h = n % heads
            s1 = st_ref[2 * h, tt]
            s2 = st_ref[2 * h + 1, tt]
            r1 = rank_ref[2 * h, tt]
            cnt = jnp.zeros((nk, LANES), F32)
            for a in range(topk):
                cnt = jnp.where(r1 == float(a), c_ref[tt, a, pl.ds(h, 1), :], cnt)
            m1 = sv_ref[tt, 0, 0, pl.ds(h, 1), :]
            m2 = sv_ref[tt, 1, 0, pl.ds(h, 1), :]
            cnt1_ref[tt, h] = cnt
            ea1_ref[tt, h] = jnp.exp(s1 - m1) * (0.5 * iz_ref[tt, pl.ds(h, 1), :])
            rk2_ref[tt, h] = rank_ref[2 * h + 1, tt].astype(BF16)
            eb2_ref[tt, h] = jnp.exp(s2 - m2).astype(BF16)
            return carry

        lax.fori_loop(0, nt * heads, tables, 0)

    hid_ref[...] = jnp.dot(u_ref[...], xnt_ref[...], preferred_element_type=F32)
    i0 = e * (et // nk)
    zero = jnp.zeros((), BF16)

    def build(il, carry):
        i = i0 + il
        for tt in range(nt):
            lanes = slice(tt * LANES, (tt + 1) * LANES)
            c1 = [jnp.broadcast_to(cnt1_ref[tt, h, pl.ds(i, 1), :], (BF16_ROWS, LANES)).astype(BF16)
                  for h in range(heads)]
            a1 = [jnp.broadcast_to(ea1_ref[tt, h, pl.ds(i, 1), :], (BF16_ROWS, LANES)).astype(BF16)
                  for h in range(heads)]
            for jb in range(nk // BF16_ROWS):
                jrows = slice(jb * BF16_ROWS, (jb + 1) * BF16_ROWS)
                w = jnp.zeros((BF16_ROWS, LANES), BF16)
                for h in range(heads):
                    w = w + jnp.where(c1[h] > rk2_ref[tt, h, jrows, :], a1[h] * eb2_ref[tt, h, jrows, :], zero)
                rows = pl.ds(pl.multiple_of(il * nk + jb * BF16_ROWS, BF16_ROWS), BF16_ROWS)
                hv = hid_ref[rows, lanes].astype(BF16)
                pt_ref[rows, lanes] = w * hv * (1.0 + lax.erf(hv * (2.0 ** -0.5)))
        return carry

    lax.fori_loop(0, et // nk, build, 0)
    acc_ref[...] += jnp.dot(vt_ref[...], pt_ref[...], preferred_element_type=F32)

    @pl.when(e == n_e - 1)
    def _finish():
        out = x_ref[...] + acc_ref[...].T
        if final:
            out = _rmsnorm_rows(out, fg_ref[...])
        o_ref[...] = out


def _peer(xf, g, wqt, sk, u, vt, final_g=None, *, tb, et):
    t, d = xf.shape
    ne = u.shape[0]
    n_groups, nk, _ = sk.shape
    heads = n_groups // 2
    nt = tb // LANES
    final = final_g is not None
    tok = pl.BlockSpec((tb, d), lambda ti, e: (ti, 0))
    in_specs = [tok, _const_spec((1, d)), _const_spec(wqt.shape), _const_spec(sk.shape),
                pl.BlockSpec((et, d), lambda ti, e: (e, 0)),
                pl.BlockSpec((d, et), lambda ti, e: (0, e))]
    args = [xf, g, wqt, sk, u, vt]
    if final:
        in_specs.append(_const_spec((1, d)))
        args.append(final_g)
    scratch = [pltpu.VMEM((d, tb), BF16),
               pltpu.VMEM((n_groups, nt, nk, LANES), F32),
               pltpu.VMEM((n_groups, nt, nk, LANES), F32),
               pltpu.VMEM((nt, 2, PEER_TOPK, heads, LANES), F32),
               pltpu.VMEM((nt, PEER_TOPK, heads, LANES), F32),
               pltpu.VMEM((nt, heads, LANES), F32),
               pltpu.VMEM((nt, heads, nk, LANES), F32),
               pltpu.VMEM((nt, heads, nk, LANES), F32),
               pltpu.VMEM((nt, heads, nk, LANES), BF16),
               pltpu.VMEM((nt, heads, nk, LANES), BF16),
               pltpu.VMEM((et, tb), F32),
               pltpu.VMEM((et, tb), BF16),
               pltpu.VMEM((d, tb), F32)]
    return pl.pallas_call(
        functools.partial(_peer_body, heads=heads, topk=PEER_TOPK, final=final),
        grid=(t // tb, ne // et),
        in_specs=in_specs,
        out_specs=tok,
        out_shape=jax.ShapeDtypeStruct((t, d), F32),
        scratch_shapes=scratch,
        compiler_params=_params("parallel", "arbitrary"),
        name="peer_final" if final else "peer",
    )(*args)


SEQ_TILE = 512
PEER_TOKENS = 512
PEER_EXPERTS = 1024


def _rope_tables(seq_len, half):
    freqs = 1.0 / (ROPE_BASE ** (jnp.arange(half, dtype=F32) / half))
    ang = jnp.arange(seq_len, dtype=F32)[:, None] * freqs[None, :]
    return jnp.cos(ang), jnp.sin(ang)


def _trunk(x, p):
    b, l, d = x.shape
    ts = min(SEQ_TILE, l)
    tb = min(PEER_TOKENS, b * l)
    et = min(PEER_EXPERTS, p["u"].shape[1])
    assert l % ts == 0 and ts % CHUNK == 0 and ts % CONV_ROWS == 0 and (b * l) % tb == 0 and tb % LANES == 0
    cos, sin = _rope_tables(l, d // RET_HEADS // 2)
    depth = p["w_in"].shape[0]
    for li in range(depth):
        row = lambda a: a[li][None, :]
        u, q, k, v, gs = _inproj(x.reshape(b * l, d), row(p["norm_mix_g"]), p["w_in"][li], cos, sin,
                                 seq_len=l, ts=ts)
        u, q, k, v, gs = (a.reshape(b, l, d) for a in (u, q, k, v, gs))
        lgf, lgb = p["lgf"][li], p["lgb"][li]
        cb = _retention(lgf, lgb, q, k, v, ts=ts)
        r = _retention(lgf, lgb, q, k, v, cb, gs, ts=ts)
        x = _post(x, u, r, row(p["norm_mix_g"]), p["conv_w"][li], row(p["conv_b"]), row(p["conv_ln_g"]),
                  row(p["conv_ln_b"]), p["w_conv_out"][li], p["w_ret_out"][li], p["w_gate"][li],
                  p["w_o"][li], ts=ts)
        fg = p["final_norm_g"][None, :] if li == depth - 1 else None
        x = _peer(x.reshape(b * l, d), row(p["norm_ffn_g"]), p["wqt"][li], p["sk"][li], p["u"][li],
                  p["vt"][li], fg, tb=tb, et=et).reshape(b, l, d)
    return x


def kernel(x_prompt, x_sample, norm_mix_g, w_in, w_gate, conv_w, conv_b, conv_ln_g, conv_ln_b, w_conv_out, log_gamma_fwd, log_gamma_bwd, w_ret_out, w_o, norm_ffn_g, peer_wq, peer_subkeys, peer_u, peer_v, final_norm_g):
    d = x_prompt.shape[-1]
    depth, n_heads, _, nk, dq = peer_subkeys.shape
    assert w_in.shape[-1] == 6 * d and conv_w.shape[1] == CONV_KERNEL
    p = dict(
        norm_mix_g=norm_mix_g, norm_ffn_g=norm_ffn_g, final_norm_g=final_norm_g,
        conv_w=conv_w, conv_b=conv_b, conv_ln_g=conv_ln_g, conv_ln_b=conv_ln_b,
        lgf=log_gamma_fwd.astype(F32), lgb=log_gamma_bwd.astype(F32),
        w_in=w_in.astype(BF16), w_gate=w_gate.astype(BF16), w_conv_out=w_conv_out.astype(BF16),
        w_ret_out=w_ret_out.astype(BF16), w_o=w_o.astype(BF16),
        wqt=jnp.swapaxes(peer_wq, 1, 2).astype(BF16),
        sk=peer_subkeys.reshape(depth, n_heads * 2, nk, dq).astype(BF16),
        u=peer_u.astype(BF16), vt=jnp.swapaxes(peer_v, 1, 2).astype(BF16),
    )
    return _trunk(x_prompt, p), _trunk(x_sample, p)
```

```python
import functools
import math

import jax
import jax.numpy as jnp
from jax import lax
from jax.experimental import pallas as pl
from jax.experimental.pallas import tpu as pltpu

F32 = jnp.float32
BF16 = jnp.bfloat16

EPS = 1e-6
RET_HEADS = 4
CHUNK = 128
ROPE_BASE = 10000.0
CONV_KERNEL = 31
PEER_TOPK = 16

LANES = 128
SUBLANES = 8
BF16_ROWS = 16
CONV_HALO = 16
VMEM_LIMIT_BYTES = 56 * 1024 * 1024


def _const_spec(shape):
    nd = len(shape)
    return pl.BlockSpec(shape, lambda *_: (0,) * nd, pipeline_mode=pl.Buffered(1))


def _params(*sem):
    return pltpu.CompilerParams(dimension_semantics=sem, vmem_limit_bytes=VMEM_LIMIT_BYTES)


def _rmsnorm_rows(x, g):
    return x * lax.rsqrt(jnp.mean(x * x, axis=-1, keepdims=True) + EPS) * g


def _inproj_body(x_ref, g_ref, w_ref, cos_ref, sin_ref, u_ref, q_ref, k_ref, v_ref, gs_ref, *, heads):
    x = x_ref[...]
    d = x.shape[-1]
    xn = _rmsnorm_rows(x, g_ref[...]).astype(BF16)

    def seg(n):
        return jnp.dot(xn, w_ref[:, n * d:(n + 1) * d], preferred_element_type=F32)

    u_ref[...] = (seg(0) * jax.nn.sigmoid(seg(1))).astype(BF16)
    cos = cos_ref[...]
    sin = sin_ref[...]
    dk = d // heads
    half = dk // 2

    def rope_store(z, o_ref, scale):
        for h in range(heads):
            lo = slice(h * dk, h * dk + half)
            hi = slice(h * dk + half, (h + 1) * dk)
            t1 = z[:, lo]
            t2 = z[:, hi]
            o_ref[:, lo] = ((t1 * cos - t2 * sin) * scale).astype(BF16)
            o_ref[:, hi] = ((t1 * sin + t2 * cos) * scale).astype(BF16)

    rope_store(seg(2), q_ref, 1.0)
    rope_store(seg(3), k_ref, float(dk) ** -0.5)
    v_ref[...] = seg(4).astype(BF16)
    gs_ref[...] = jax.nn.silu(seg(5)).astype(BF16)


def _inproj(xf, g, w_in, cos, sin, *, seq_len, ts):
    t, d = xf.shape
    half = cos.shape[-1]
    n_seq_tiles = seq_len // ts
    tok = pl.BlockSpec((ts, d), lambda i: (i, 0))
    pos = pl.BlockSpec((ts, half), lambda i: (i % n_seq_tiles, 0))
    out = jax.ShapeDtypeStruct((t, d), BF16)
    return pl.pallas_call(
        functools.partial(_inproj_body, heads=RET_HEADS),
        grid=(t // ts,),
        in_specs=[tok, _const_spec((1, d)), _const_spec(w_in.shape), pos, pos],
        out_specs=[tok] * 5,
        out_shape=[out] * 5,
        compiler_params=_params("parallel"),
        name="inproj",
    )(xf, g, w_in, cos, sin)


def _retention_body(lgf_ref, lgb_ref, q_ref, k_ref, v_ref, *rest, heads, chunk, final):
    if final:
        cb_ref, gs_ref, o_ref, s_ref, qd_ref, kd_ref, cd_ref, dm_ref = rest
    else:
        o_ref, s_ref, qd_ref, kd_ref, cd_ref = rest
        dm_ref = None
    d = q_ref.shape[-1]
    dk = d // heads
    c = chunk
    n_chunks = q_ref.shape[1] // c

    @pl.when(pl.program_id(1) == 0)
    def _():
        s_ref[...] = jnp.zeros(s_ref.shape, F32)
        row = lax.broadcasted_iota(jnp.int32, (c, dk), 0).astype(F32)
        ri = lax.broadcasted_iota(jnp.int32, (c, c), 0)
        ci = lax.broadcasted_iota(jnp.int32, (c, c), 1)
        for h in range(heads):
            cols = slice(h * dk, (h + 1) * dk)
            if final:
                lg = lgf_ref[h]
                qd_ref[:, cols] = jnp.exp((row + 1.0) * lg)
                kd_ref[:, cols] = jnp.exp((c - 1.0 - row) * lg)
                lower = ri >= ci
                diff = jnp.where(lower, ri - ci, ci - ri).astype(F32)
                dm_ref[h] = jnp.exp(diff * jnp.where(lower, lg, lgb_ref[h]))
            else:
                lg = lgb_ref[h]
                qd_ref[:, cols] = jnp.exp((float(c) - row) * lg)
                kd_ref[:, cols] = jnp.exp(row * lg)
            cd_ref[:, cols] = jnp.exp(jnp.full((1, dk), float(c), F32) * lg)

    order = range(n_chunks) if final else range(n_chunks - 1, -1, -1)
    for n in order:
        rows = slice(n * c, (n + 1) * c)
        for h in range(heads):
            cols = slice(h * dk, (h + 1) * dk)
            qc = q_ref[0, rows, cols]
            kc = k_ref[0, rows, cols]
            vc = v_ref[0, rows, cols]
            state = s_ref[h]
            qdec = (qc.astype(F32) * qd_ref[:, cols]).astype(BF16)
            kdec = (kc.astype(F32) * kd_ref[:, cols]).astype(BF16)
            cross = jnp.dot(qdec, state.astype(BF16), preferred_element_type=F32)
            s_ref[h] = state * cd_ref[:, cols] + lax.dot_general(
                kdec, vc, (((0,), (0,)), ((), ())), preferred_element_type=F32)
            if final:
                scores = lax.dot_general(qc, kc, (((1,), (1,)), ((), ())), preferred_element_type=F32)
                intra = jnp.dot((scores * dm_ref[h]).astype(BF16), vc, preferred_element_type=F32)
                o = intra + cross + cb_ref[0, rows, cols].astype(F32)
                o = o * lax.rsqrt(jnp.mean(o * o, axis=-1, keepdims=True) + EPS)
                o_ref[0, rows, cols] = (gs_ref[0, rows, cols].astype(F32) * o).astype(BF16)
            else:
                o_ref[0, rows, cols] = cross.astype(BF16)


def _retention(lgf, lgb, q, k, v, cb=None, gs=None, *, ts):
    b, l, d = q.shape
    final = cb is not None
    nb = l // ts
    dk = d // RET_HEADS
    if final:
        blk = pl.BlockSpec((1, ts, d), lambda bi, j: (bi, j, 0))
    else:
        blk = pl.BlockSpec((1, ts, d), lambda bi, j: (bi, nb - 1 - j, 0))
    smem = pl.BlockSpec(memory_space=pltpu.SMEM)
    scratch = [pltpu.VMEM((RET_HEADS, dk, dk), F32),
               pltpu.VMEM((CHUNK, d), F32),
               pltpu.VMEM((CHUNK, d), F32),
               pltpu.VMEM((1, d), F32)]
    args = [lgf, lgb, q, k, v]
    in_specs = [smem, smem, blk, blk, blk]
    if final:
        scratch.append(pltpu.VMEM((RET_HEADS, CHUNK, CHUNK), F32))
        args += [cb, gs]
        in_specs += [blk, blk]
    return pl.pallas_call(
        functools.partial(_retention_body, heads=RET_HEADS, chunk=CHUNK, final=final),
        grid=(b, nb),
        in_specs=in_specs,
        out_specs=blk,
        out_shape=jax.ShapeDtypeStruct((b, l, d), BF16),
        scratch_shapes=scratch,
        compiler_params=_params("arbitrary", "arbitrary"),
        name="retention_fwd" if final else "retention_bwd",
    )(*args)


CONV_ROWS = 64
CONV_COLS = 128


def _post_body(x_ref, up_ref, uc_ref, un_ref, r_ref, gmix_ref, cw_ref, cbias_ref, lng_ref, lnb_ref,
               wco_ref, wro_ref, wg_ref, wo_ref, o_ref, win_ref, y_ref, *, n_tiles):
    i = pl.program_id(1)
    ts, d = uc_ref.shape[1], uc_ref.shape[2]
    halo = up_ref.shape[1]
    n_win = ts // CONV_ROWS
    taps = cw_ref.shape[0]
    pad = taps // 2

    prev = jnp.where(i > 0, up_ref[0].astype(F32), 0.0)
    nxt = jnp.where(i < n_tiles - 1, un_ref[0].astype(F32), 0.0)
    for w in range(n_win):
        lo = w * CONV_ROWS
        win_ref[w, halo:halo + CONV_ROWS, :] = uc_ref[0, lo:lo + CONV_ROWS, :].astype(F32)
        if w == 0:
            win_ref[w, 0:halo, :] = prev
        else:
            win_ref[w, 0:halo, :] = uc_ref[0, lo - halo:lo, :].astype(F32)
        if w == n_win - 1:
            win_ref[w, halo + CONV_ROWS:, :] = nxt
        else:
            win_ref[w, halo + CONV_ROWS:, :] = uc_ref[0, lo + CONV_ROWS:lo + CONV_ROWS + halo, :].astype(F32)

    def conv_rows(w, carry):
        for cb in range(d // CONV_COLS):
            cols = slice(cb * CONV_COLS, (cb + 1) * CONV_COLS)
            window = win_ref[w, :, cols]
            n_rows = CONV_ROWS + 2 * halo
            phase = [window] + [pltpu.roll(window, n_rows - s, 0) for s in range(1, SUBLANES)]
            acc = jnp.broadcast_to(cbias_ref[:, cols], (CONV_ROWS, CONV_COLS))
            for t in range(taps):
                q, s = divmod(halo - pad + t, SUBLANES)
                acc = acc + cw_ref[t:t + 1, cols] * phase[s][q * SUBLANES:q * SUBLANES + CONV_ROWS]
            y_ref[pl.ds(pl.multiple_of(w * CONV_ROWS, CONV_ROWS), CONV_ROWS), cols] = acc
        return carry

    lax.fori_loop(0, n_win, conv_rows, 0)

    y = y_ref[...]
    mu = jnp.mean(y, axis=-1, keepdims=True)
    yc = y - mu
    var = jnp.mean(yc * yc, axis=-1, keepdims=True)
    yn = yc * lax.rsqrt(var + EPS) * lng_ref[...] + lnb_ref[...]
    y_conv = jnp.dot(jax.nn.silu(yn).astype(BF16), wco_ref[...], preferred_element_type=F32)
    y_ret = jnp.dot(r_ref[0], wro_ref[...], preferred_element_type=F32)
    x = x_ref[0]
    xn = _rmsnorm_rows(x, gmix_ref[...]).astype(BF16)
    gates = jax.nn.sigmoid(jnp.dot(xn, wg_ref[...], preferred_element_type=F32))
    merged = gates[:, :d] * y_conv + gates[:, d:] * y_ret
    o_ref[0] = x + jnp.dot(merged.astype(BF16), wo_ref[...], preferred_element_type=F32)


def _post(x, u, r, gmix, conv_w, conv_b, ln_g, ln_b, w_conv_out, w_ret_out, w_gate, w_o, *, ts):
    b, l, d = x.shape
    n_tiles = l // ts
    hpt = ts // CONV_HALO
    n_halo = l // CONV_HALO
    tile = pl.BlockSpec((1, ts, d), lambda bi, i: (bi, i, 0))
    prev = pl.BlockSpec((1, CONV_HALO, d), lambda bi, i: (bi, jnp.maximum(i * hpt - 1, 0), 0))
    nxt = pl.BlockSpec((1, CONV_HALO, d), lambda bi, i: (bi, jnp.minimum((i + 1) * hpt, n_halo - 1), 0))
    vec = _const_spec((1, d))
    return pl.pallas_call(
        functools.partial(_post_body, n_tiles=n_tiles),
        grid=(b, n_tiles),
        in_specs=[tile, prev, tile, nxt, tile, vec, _const_spec(conv_w.shape), vec, vec, vec,
                  _const_spec(w_conv_out.shape), _const_spec(w_ret_out.shape),
                  _const_spec(w_gate.shape), _const_spec(w_o.shape)],
        out_specs=tile,
        out_shape=jax.ShapeDtypeStruct((b, l, d), F32),
        scratch_shapes=[pltpu.VMEM((ts // CONV_ROWS, CONV_ROWS + 2 * CONV_HALO, d), F32),
                        pltpu.VMEM((ts, d), F32)],
        compiler_params=_params("parallel", "parallel"),
        name="post",
    )(x, u, u, u, r, gmix, conv_w, conv_b, ln_g, ln_b, w_conv_out, w_ret_out, w_gate, w_o)


def _peer_body(x_ref, g_ref, wqt_ref, sk_ref, u_ref, vt_ref, *rest, heads, topk, final):
    if final:
        fg_ref = rest[0]
        rest = rest[1:]
    (o_ref, xnt_ref, st_ref, rank_ref, sv_ref, c_ref, iz_ref, cnt1_ref, ea1_ref, eb2_ref,
     hid_ref, pt_ref, acc_ref) = rest
    e = pl.program_id(1)
    n_e = pl.num_programs(1)
    tb, d = x_ref.shape
    n_groups, nk, dq = sk_ref.shape
    nt = tb // LANES
    et = u_ref.shape[0]
    neg_inf = float("-inf")

    @pl.when(e == 0)
    def _routing():
        xn = _rmsnorm_rows(x_ref[...], g_ref[...])
        xnt = xn.T.astype(BF16)
        xnt_ref[...] = xnt
        qt = jnp.dot(wqt_ref[...], xnt, preferred_element_type=F32).astype(BF16)
        for g in range(n_groups):
            s = jnp.dot(sk_ref[g], qt[g * dq:(g + 1) * dq, :], preferred_element_type=F32)
            for tt in range(nt):
                st_ref[g, tt] = s[:, tt * LANES:(tt + 1) * LANES]
        acc_ref[...] = jnp.zeros(acc_ref.shape, F32)

        def extract(n, carry):
            h = n // nt
            tt = n % nt
            s = [st_ref[2 * h + p, tt] for p in range(2)]
            rank = [jnp.full((nk, LANES), float(topk), F32) for _ in range(2)]
            for r in range(topk):
                for p in range(2):
                    m = jnp.max(s[p], axis=0, keepdims=True)
                    eq = s[p] == m
                    rank[p] = jnp.where(eq, float(r), rank[p])
                    s[p] = jnp.where(eq, neg_inf, s[p])
                    sv_ref[tt, p, r, pl.ds(h, 1), :] = m
            for p in range(2):
                rank_ref[2 * h + p, tt] = rank[p].astype(BF16)
            return carry

        lax.fori_loop(0, heads * nt, extract, 0)

        pairs = [(a, b) for a in range(topk) for b in range(topk) if (a + 1) * (b + 1) <= topk]

        def select(tt, carry):
            sv1 = [sv_ref[tt, 0, r] for r in range(topk)]
            sv2 = [sv_ref[tt, 1, r] for r in range(topk)]
            cand = [sv1[a] + sv2[b] for a, b in pairs]
            work = list(cand)
            tau = None
            for it in range(topk):
                tau = functools.reduce(jnp.maximum, work)
                if it < topk - 1:
                    work = [jnp.where(w == tau, neg_inf, w) for w in work]
            ea = [jnp.exp(v - sv1[0]) for v in sv1]
            eb = [jnp.exp(v - sv2[0]) for v in sv2]
            cnt = [jnp.zeros((heads, LANES), F32) for _ in range(topk)]
            z = jnp.zeros((heads, LANES), F32)
            for (a, b), cv in zip(pairs, cand):
                sel = cv >= tau
                cnt[a] = cnt[a] + jnp.where(sel, 1.0, 0.0)
                z = z + jnp.where(sel, ea[a] * eb[b], 0.0)
            for a in range(topk):
                c_ref[tt, a] = cnt[a]
            iz_ref[tt] = 1.0 / z
            return carry

        lax.fori_loop(0, nt, select, 0)

        def twice(v):
            bits = lax.bitcast_convert_type(v.astype(BF16).astype(F32), jnp.uint32)
            return bits | (bits >> 16)

        def tables(n, carry):
            tt = n // heads
            h = n % heads
            s1 = st_ref[2 * h, tt]
            s2 = st_ref[2 * h + 1, tt]
            r1 = rank_ref[2 * h, tt].astype(F32)
            cnt = jnp.zeros((nk, LANES), F32)
            for a in range(topk):
                cnt = jnp.where(r1 == float(a), c_ref[tt, a, pl.ds(h, 1), :], cnt)
            m1 = sv_ref[tt, 0, 0, pl.ds(h, 1), :]
            m2 = sv_ref[tt, 1, 0, pl.ds(h, 1), :]
            cnt1_ref[tt, h] = twice(cnt)
            ea1_ref[tt, h] = twice(jnp.exp(s1 - m1) * (0.5 * iz_ref[tt, pl.ds(h, 1), :]))
            eb2_ref[tt, h] = jnp.exp(s2 - m2).astype(BF16)
            return carry

        lax.fori_loop(0, nt * heads, tables, 0)

    i0 = e * (et // nk)
    zero = jnp.zeros((), BF16)
    xnt = xnt_ref[...]
    acc = acc_ref[...]
    n_sub = et // PEER_SUB

    def hidden(k):
        sub = slice(k * PEER_SUB, (k + 1) * PEER_SUB)
        hid_ref[k % 2] = jnp.dot(u_ref[sub, :], xnt, preferred_element_type=F32)

    def bcast_row(tab_ref, tt, h, i):
        word_row = tab_ref[tt, h, pl.ds(i, 1), :]
        return pltpu.bitcast(jnp.broadcast_to(word_row, (BF16_ROWS // 2, LANES)), BF16)

    hidden(0)
    for k in range(n_sub):
        sub = slice(k * PEER_SUB, (k + 1) * PEER_SUB)
        slot = k % 2
        if k + 1 < n_sub:
            hidden(k + 1)
        for tt in range(nt):
            lanes = slice(tt * LANES, (tt + 1) * LANES)
            for jb in range(nk // BF16_ROWS):
                jrows = slice(jb * BF16_ROWS, (jb + 1) * BF16_ROWS)
                r2 = [rank_ref[2 * h + 1, tt, jrows, :] for h in range(heads)]
                b2 = [eb2_ref[tt, h, jrows, :] for h in range(heads)]
                for il in range(PEER_SUB // nk):
                    i = i0 + k * (PEER_SUB // nk) + il
                    w = jnp.zeros((BF16_ROWS, LANES), BF16)
                    for h in range(heads):
                        w = w + jnp.where(bcast_row(cnt1_ref, tt, h, i) > r2[h],
                                          bcast_row(ea1_ref, tt, h, i) * b2[h], zero)
                    rows = slice(il * nk + jb * BF16_ROWS, il * nk + (jb + 1) * BF16_ROWS)
                    hv = hid_ref[slot, rows, lanes].astype(BF16)
                    pt_ref[slot, rows, lanes] = w * hv * (1.0 + lax.erf(hv * (2.0 ** -0.5)))
        acc = acc + jnp.dot(vt_ref[:, sub], pt_ref[slot], preferred_element_type=F32)
    acc_ref[...] = acc

    @pl.when(e == n_e - 1)
    def _finish():
        out = x_ref[...] + acc_ref[...].T
        if final:
            out = _rmsnorm_rows(out, fg_ref[...])
        o_ref[...] = out


def _peer(xf, g, wqt, sk, u, vt, final_g=None, *, tb, et):
    t, d = xf.shape
    ne = u.shape[0]
    n_groups, nk, _ = sk.shape
    heads = n_groups // 2
    nt = tb // LANES
    final = final_g is not None
    tok = pl.BlockSpec((tb, d), lambda ti, e: (ti, 0))
    in_specs = [tok, _const_spec((1, d)), _const_spec(wqt.shape), _const_spec(sk.shape),
                pl.BlockSpec((et, d), lambda ti, e: (e, 0)),
                pl.BlockSpec((d, et), lambda ti, e: (0, e))]
    args = [xf, g, wqt, sk, u, vt]
    if final:
        in_specs.append(_const_spec((1, d)))
        args.append(final_g)
    scratch = [pltpu.VMEM((d, tb), BF16),
               pltpu.VMEM((n_groups, nt, nk, LANES), F32),
               pltpu.VMEM((n_groups, nt, nk, LANES), BF16),
               pltpu.VMEM((nt, 2, PEER_TOPK, heads, LANES), F32),
               pltpu.VMEM((nt, PEER_TOPK, heads, LANES), F32),
               pltpu.VMEM((nt, heads, LANES), F32),
               pltpu.VMEM((nt, heads, nk, LANES), jnp.uint32),
               pltpu.VMEM((nt, heads, nk, LANES), jnp.uint32),
               pltpu.VMEM((nt, heads, nk, LANES), BF16),
               pltpu.VMEM((2, PEER_SUB, tb), F32),
               pltpu.VMEM((2, PEER_SUB, tb), BF16),
               pltpu.VMEM((d, tb), F32)]
    return pl.pallas_call(
        functools.partial(_peer_body, heads=heads, topk=PEER_TOPK, final=final),
        grid=(t // tb, ne // et),
        in_specs=in_specs,
        out_specs=tok,
        out_shape=jax.ShapeDtypeStruct((t, d), F32),
        scratch_shapes=scratch,
        compiler_params=_params("parallel", "arbitrary"),
        name="peer_final" if final else "peer",
    )(*args)


SEQ_TILE = 512
PEER_TOKENS = 512
PEER_EXPERTS = 1024
PEER_SUB = 256


def _rope_tables(seq_len, half):
    freqs = 1.0 / (ROPE_BASE ** (jnp.arange(half, dtype=F32) / half))
    ang = jnp.arange(seq_len, dtype=F32)[:, None] * freqs[None, :]
    return jnp.cos(ang), jnp.sin(ang)


def _trunk(x, p):
    b, l, d = x.shape
    ts = min(SEQ_TILE, l)
    tb = min(PEER_TOKENS, b * l)
    et = min(PEER_EXPERTS, p["u"].shape[1])
    assert l % ts == 0 and ts % CHUNK == 0 and ts % CONV_ROWS == 0 and (b * l) % tb == 0 and tb % LANES == 0
    cos, sin = _rope_tables(l, d // RET_HEADS // 2)
    depth = p["w_in"].shape[0]
    for li in range(depth):
        row = lambda a: a[li][None, :]
        u, q, k, v, gs = _inproj(x.reshape(b * l, d), row(p["norm_mix_g"]), p["w_in"][li], cos, sin,
                                 seq_len=l, ts=ts)
        u, q, k, v, gs = (a.reshape(b, l, d) for a in (u, q, k, v, gs))
        lgf, lgb = p["lgf"][li], p["lgb"][li]
        cb = _retention(lgf, lgb, q, k, v, ts=ts)
        r = _retention(lgf, lgb, q, k, v, cb, gs, ts=ts)
        x = _post(x, u, r, row(p["norm_mix_g"]), p["conv_w"][li], row(p["conv_b"]), row(p["conv_ln_g"]),
                  row(p["conv_ln_b"]), p["w_conv_out"][li], p["w_ret_out"][li], p["w_gate"][li],
                  p["w_o"][li], ts=ts)
        fg = p["final_norm_g"][None, :] if li == depth - 1 else None
        x = _peer(x.reshape(b * l, d), row(p["norm_ffn_g"]), p["wqt"][li], p["sk"][li], p["u"][li],
                  p["vt"][li], fg, tb=tb, et=et).reshape(b, l, d)
    return x


def kernel(x_prompt, x_sample, norm_mix_g, w_in, w_gate, conv_w, conv_b, conv_ln_g, conv_ln_b, w_conv_out, log_gamma_fwd, log_gamma_bwd, w_ret_out, w_o, norm_ffn_g, peer_wq, peer_subkeys, peer_u, peer_v, final_norm_g):
    d = x_prompt.shape[-1]
    depth, n_heads, _, nk, dq = peer_subkeys.shape
    assert w_in.shape[-1] == 6 * d and conv_w.shape[1] == CONV_KERNEL
    p = dict(
        norm_mix_g=norm_mix_g, norm_ffn_g=norm_ffn_g, final_norm_g=final_norm_g,
        conv_w=conv_w, conv_b=conv_b, conv_ln_g=conv_ln_g, conv_ln_b=conv_ln_b,
        lgf=log_gamma_fwd.astype(F32), lgb=log_gamma_bwd.astype(F32),
        w_in=w_in.astype(BF16), w_gate=w_gate.astype(BF16), w_conv_out=w_conv_out.astype(BF16),
        w_ret_out=w_ret_out.astype(BF16), w_o=w_o.astype(BF16),
        wqt=jnp.swapaxes(peer_wq, 1, 2).astype(BF16),
        sk=peer_subkeys.reshape(depth, n_heads * 2, nk, dq).astype(BF16),
        u=peer_u.astype(BF16), vt=jnp.swapaxes(peer_v, 1, 2).astype(BF16),
    )
    return _trunk(x_prompt, p), _trunk(x_sample, p)
```

```python
import functools
import math

import jax
import jax.numpy as jnp
from jax import lax
from jax.experimental import pallas as pl
from jax.experimental.pallas import tpu as pltpu

F32 = jnp.float32
BF16 = jnp.bfloat16

EPS = 1e-6
RET_HEADS = 4
CHUNK = 128
ROPE_BASE = 10000.0
CONV_KERNEL = 31
PEER_TOPK = 16

LANES = 128
SUBLANES = 8
BF16_ROWS = 16
CONV_HALO = 16
VMEM_LIMIT_BYTES = 56 * 1024 * 1024


def _const_spec(shape):
    nd = len(shape)
    return pl.BlockSpec(shape, lambda *_: (0,) * nd, pipeline_mode=pl.Buffered(1))


def _params(*sem):
    return pltpu.CompilerParams(dimension_semantics=sem, vmem_limit_bytes=VMEM_LIMIT_BYTES)


def _rmsnorm_rows(x, g):
    return x * lax.rsqrt(jnp.mean(x * x, axis=-1, keepdims=True) + EPS) * g


def _inproj_body(x_ref, g_ref, w_ref, cos_ref, sin_ref, u_ref, q_ref, k_ref, v_ref, gs_ref, *, heads):
    x = x_ref[...]
    d = x.shape[-1]
    xn = _rmsnorm_rows(x, g_ref[...]).astype(BF16)

    def seg(n):
        return jnp.dot(xn, w_ref[:, n * d:(n + 1) * d], preferred_element_type=F32)

    u_ref[...] = (seg(0) * jax.nn.sigmoid(seg(1))).astype(BF16)
    cos = cos_ref[...]
    sin = sin_ref[...]
    dk = d // heads
    half = dk // 2

    def rope_store(z, o_ref, scale):
        for h in range(heads):
            lo = slice(h * dk, h * dk + half)
            hi = slice(h * dk + half, (h + 1) * dk)
            t1 = z[:, lo]
            t2 = z[:, hi]
            o_ref[:, lo] = ((t1 * cos - t2 * sin) * scale).astype(BF16)
            o_ref[:, hi] = ((t1 * sin + t2 * cos) * scale).astype(BF16)

    rope_store(seg(2), q_ref, 1.0)
    rope_store(seg(3), k_ref, float(dk) ** -0.5)
    v_ref[...] = seg(4).astype(BF16)
    gs_ref[...] = jax.nn.silu(seg(5)).astype(BF16)


def _inproj(xf, g, w_in, cos, sin, *, seq_len, ts):
    t, d = xf.shape
    half = cos.shape[-1]
    n_seq_tiles = seq_len // ts
    tok = pl.BlockSpec((ts, d), lambda i: (i, 0))
    pos = pl.BlockSpec((ts, half), lambda i: (i % n_seq_tiles, 0))
    out = jax.ShapeDtypeStruct((t, d), BF16)
    return pl.pallas_call(
        functools.partial(_inproj_body, heads=RET_HEADS),
        grid=(t // ts,),
        in_specs=[tok, _const_spec((1, d)), _const_spec(w_in.shape), pos, pos],
        out_specs=[tok] * 5,
        out_shape=[out] * 5,
        compiler_params=_params("parallel"),
        name="inproj",
    )(xf, g, w_in, cos, sin)


def _retention_body(lgf_ref, lgb_ref, q_ref, k_ref, v_ref, *rest, heads, chunk, final):
    if final:
        cb_ref, gs_ref, o_ref, s_ref, qd_ref, kd_ref, cd_ref, dm_ref = rest
    else:
        o_ref, s_ref, qd_ref, kd_ref, cd_ref = rest
        dm_ref = None
    d = q_ref.shape[-1]
    dk = d // heads
    c = chunk
    n_chunks = q_ref.shape[1] // c

    @pl.when(pl.program_id(1) == 0)
    def _():
        s_ref[...] = jnp.zeros(s_ref.shape, F32)
        row = lax.broadcasted_iota(jnp.int32, (c, dk), 0).astype(F32)
        ri = lax.broadcasted_iota(jnp.int32, (c, c), 0)
        ci = lax.broadcasted_iota(jnp.int32, (c, c), 1)
        for h in range(heads):
            cols = slice(h * dk, (h + 1) * dk)
            if final:
                lg = lgf_ref[h]
                qd_ref[:, cols] = jnp.exp((row + 1.0) * lg)
                kd_ref[:, cols] = jnp.exp((c - 1.0 - row) * lg)
                lower = ri >= ci
                diff = jnp.where(lower, ri - ci, ci - ri).astype(F32)
                dm_ref[h] = jnp.exp(diff * jnp.where(lower, lg, lgb_ref[h]))
            else:
                lg = lgb_ref[h]
                qd_ref[:, cols] = jnp.exp((float(c) - row) * lg)
                kd_ref[:, cols] = jnp.exp(row * lg)
            cd_ref[:, cols] = jnp.exp(jnp.full((1, dk), float(c), F32) * lg)

    order = range(n_chunks) if final else range(n_chunks - 1, -1, -1)
    for n in order:
        rows = slice(n * c, (n + 1) * c)
        for h in range(heads):
            cols = slice(h * dk, (h + 1) * dk)
            qc = q_ref[0, rows, cols]
            kc = k_ref[0, rows, cols]
            vc = v_ref[0, rows, cols]
            state = s_ref[h]
            qdec = (qc.astype(F32) * qd_ref[:, cols]).astype(BF16)
            kdec = (kc.astype(F32) * kd_ref[:, cols]).astype(BF16)
            cross = jnp.dot(qdec, state.astype(BF16), preferred_element_type=F32)
            s_ref[h] = state * cd_ref[:, cols] + lax.dot_general(
                kdec, vc, (((0,), (0,)), ((), ())), preferred_element_type=F32)
            if final:
                scores = lax.dot_general(qc, kc, (((1,), (1,)), ((), ())), preferred_element_type=F32)
                intra = jnp.dot((scores * dm_ref[h]).astype(BF16), vc, preferred_element_type=F32)
                o = intra + cross + cb_ref[0, rows, cols].astype(F32)
                o = o * lax.rsqrt(jnp.mean(o * o, axis=-1, keepdims=True) + EPS)
                o_ref[0, rows, cols] = (gs_ref[0, rows, cols].astype(F32) * o).astype(BF16)
            else:
                o_ref[0, rows, cols] = cross.astype(BF16)


def _retention(lgf, lgb, q, k, v, cb=None, gs=None, *, ts):
    b, l, d = q.shape
    final = cb is not None
    nb = l // ts
    dk = d // RET_HEADS
    if final:
        blk = pl.BlockSpec((1, ts, d), lambda bi, j: (bi, j, 0))
    else:
        blk = pl.BlockSpec((1, ts, d), lambda bi, j: (bi, nb - 1 - j, 0))
    smem = pl.BlockSpec(memory_space=pltpu.SMEM)
    scratch = [pltpu.VMEM((RET_HEADS, dk, dk), F32),
               pltpu.VMEM((CHUNK, d), F32),
               pltpu.VMEM((CHUNK, d), F32),
               pltpu.VMEM((1, d), F32)]
    args = [lgf, lgb, q, k, v]
    in_specs = [smem, smem, blk, blk, blk]
    if final:
        scratch.append(pltpu.VMEM((RET_HEADS, CHUNK, CHUNK), F32))
        args += [cb, gs]
        in_specs += [blk, blk]
    return pl.pallas_call(
        functools.partial(_retention_body, heads=RET_HEADS, chunk=CHUNK, final=final),
        grid=(b, nb),
        in_specs=in_specs,
        out_specs=blk,
        out_shape=jax.ShapeDtypeStruct((b, l, d), BF16),
        scratch_shapes=scratch,
        compiler_params=_params("arbitrary", "arbitrary"),
        name="retention_fwd" if final else "retention_bwd",
    )(*args)


CONV_ROWS = 128
CONV_COLS = 128


def _post_body(x_ref, up_ref, uc_ref, un_ref, r_ref, gmix_ref, cw_ref, cbias_ref, lng_ref, lnb_ref,
               wco_ref, wro_ref, wg_ref, wo_ref, o_ref, win_ref, y_ref, *, n_tiles):
    i = pl.program_id(1)
    ts, d = uc_ref.shape[1], uc_ref.shape[2]
    halo = up_ref.shape[1]
    n_win = ts // CONV_ROWS
    taps = cw_ref.shape[0]
    pad = taps // 2

    prev = jnp.where(i > 0, up_ref[0].astype(F32), 0.0)
    nxt = jnp.where(i < n_tiles - 1, un_ref[0].astype(F32), 0.0)
    for w in range(n_win):
        lo = w * CONV_ROWS
        win_ref[w, halo:halo + CONV_ROWS, :] = uc_ref[0, lo:lo + CONV_ROWS, :].astype(F32)
        if w == 0:
            win_ref[w, 0:halo, :] = prev
        else:
            win_ref[w, 0:halo, :] = uc_ref[0, lo - halo:lo, :].astype(F32)
        if w == n_win - 1:
            win_ref[w, halo + CONV_ROWS:, :] = nxt
        else:
            win_ref[w, halo + CONV_ROWS:, :] = uc_ref[0, lo + CONV_ROWS:lo + CONV_ROWS + halo, :].astype(F32)

    def conv_rows(w, carry):
        for cb in range(d // CONV_COLS):
            cols = slice(cb * CONV_COLS, (cb + 1) * CONV_COLS)
            window = win_ref[w, :, cols]
            n_rows = CONV_ROWS + 2 * halo
            phase = [window] + [pltpu.roll(window, n_rows - s, 0) for s in range(1, SUBLANES)]
            acc = jnp.broadcast_to(cbias_ref[:, cols], (CONV_ROWS, CONV_COLS))
            for t in range(taps):
                q, s = divmod(halo - pad + t, SUBLANES)
                acc = acc + cw_ref[t:t + 1, cols] * phase[s][q * SUBLANES:q * SUBLANES + CONV_ROWS]
            y_ref[pl.ds(pl.multiple_of(w * CONV_ROWS, CONV_ROWS), CONV_ROWS), cols] = acc
        return carry

    lax.fori_loop(0, n_win, conv_rows, 0)

    y = y_ref[...]
    mu = jnp.mean(y, axis=-1, keepdims=True)
    yc = y - mu
    var = jnp.mean(yc * yc, axis=-1, keepdims=True)
    yn = yc * lax.rsqrt(var + EPS) * lng_ref[...] + lnb_ref[...]
    y_conv = jnp.dot(jax.nn.silu(yn).astype(BF16), wco_ref[...], preferred_element_type=F32)
    y_ret = jnp.dot(r_ref[0], wro_ref[...], preferred_element_type=F32)
    x = x_ref[0]
    xn = _rmsnorm_rows(x, gmix_ref[...]).astype(BF16)
    gates = jax.nn.sigmoid(jnp.dot(xn, wg_ref[...], preferred_element_type=F32))
    merged = gates[:, :d] * y_conv + gates[:, d:] * y_ret
    o_ref[0] = x + jnp.dot(merged.astype(BF16), wo_ref[...], preferred_element_type=F32)


def _post(x, u, r, gmix, conv_w, conv_b, ln_g, ln_b, w_conv_out, w_ret_out, w_gate, w_o, *, ts):
    b, l, d = x.shape
    n_tiles = l // ts
    hpt = ts // CONV_HALO
    n_halo = l // CONV_HALO
    tile = pl.BlockSpec((1, ts, d), lambda bi, i: (bi, i, 0))
    prev = pl.BlockSpec((1, CONV_HALO, d), lambda bi, i: (bi, jnp.maximum(i * hpt - 1, 0), 0))
    nxt = pl.BlockSpec((1, CONV_HALO, d), lambda bi, i: (bi, jnp.minimum((i + 1) * hpt, n_halo - 1), 0))
    vec = _const_spec((1, d))
    return pl.pallas_call(
        functools.partial(_post_body, n_tiles=n_tiles),
        grid=(b, n_tiles),
        in_specs=[tile, prev, tile, nxt, tile, vec, _const_spec(conv_w.shape), vec, vec, vec,
                  _const_spec(w_conv_out.shape), _const_spec(w_ret_out.shape),
                  _const_spec(w_gate.shape), _const_spec(w_o.shape)],
        out_specs=tile,
        out_shape=jax.ShapeDtypeStruct((b, l, d), F32),
        scratch_shapes=[pltpu.VMEM((ts // CONV_ROWS, CONV_ROWS + 2 * CONV_HALO, d), F32),
                        pltpu.VMEM((ts, d), F32)],
        compiler_params=_params("parallel", "parallel"),
        name="post",
    )(x, u, u, u, r, gmix, conv_w, conv_b, ln_g, ln_b, w_conv_out, w_ret_out, w_gate, w_o)


def _peer_body(x_ref, g_ref, wqt_ref, sk_ref, u_ref, vt_ref, *rest, heads, topk, final):
    if final:
        fg_ref = rest[0]
        rest = rest[1:]
    (o_ref, xnt_ref, st_ref, rank1_ref, sv_ref, c_ref, iz_ref, ta_ref, tb_ref,
     hid_ref, w_ref, pt_ref, acc_ref) = rest
    e = pl.program_id(1)
    n_e = pl.num_programs(1)
    tb, d = x_ref.shape
    n_groups, nk, dq = sk_ref.shape
    nt = tb // LANES
    et = u_ref.shape[0]
    neg_inf = float("-inf")

    @pl.when(e == 0)
    def _routing():
        xn = _rmsnorm_rows(x_ref[...], g_ref[...])
        xnt = xn.T.astype(BF16)
        xnt_ref[...] = xnt
        qt = jnp.dot(wqt_ref[...], xnt, preferred_element_type=F32).astype(BF16)
        for g in range(n_groups):
            s = jnp.dot(sk_ref[g], qt[g * dq:(g + 1) * dq, :], preferred_element_type=F32)
            for tt in range(nt):
                st_ref[g, tt] = s[:, tt * LANES:(tt + 1) * LANES]
        acc_ref[...] = jnp.zeros(acc_ref.shape, F32)

        def extract(n, carry):
            h = n // nt
            tt = n % nt
            s = [st_ref[2 * h + p, tt] for p in range(2)]
            rank = [jnp.full((nk, LANES), float(topk), F32) for _ in range(2)]
            for r in range(topk):
                for p in range(2):
                    m = jnp.max(s[p], axis=0, keepdims=True)
                    eq = s[p] == m
                    rank[p] = jnp.where(eq, float(r), rank[p])
                    s[p] = jnp.where(eq, neg_inf, s[p])
                    sv_ref[tt, p, r, pl.ds(h, 1), :] = m
            rank1_ref[h, tt] = rank[0].astype(BF16)
            rank2 = rank[1].astype(BF16)
            for jb in range(nk // BF16_ROWS):
                tb_ref[tt, jb, h] = rank2[jb * BF16_ROWS:(jb + 1) * BF16_ROWS]
            return carry

        lax.fori_loop(0, heads * nt, extract, 0)

        pairs = [(a, b) for a in range(topk) for b in range(topk) if (a + 1) * (b + 1) <= topk]

        n_par = 2 if nt % 2 == 0 else 1

        def select(n, carry):
            tts = [n * n_par + u for u in range(n_par)]
            sv1 = [[sv_ref[tt, 0, r] for r in range(topk)] for tt in tts]
            sv2 = [[sv_ref[tt, 1, r] for r in range(topk)] for tt in tts]
            cand = [[sv1[u][a] + sv2[u][b] for a, b in pairs] for u in range(n_par)]
            work = [list(c) for c in cand]
            tau = [None] * n_par
            for it in range(topk):
                for u in range(n_par):
                    tau[u] = functools.reduce(jnp.maximum, work[u])
                    if it < topk - 1:
                        work[u] = [jnp.where(w == tau[u], neg_inf, w) for w in work[u]]
            for u, tt in enumerate(tts):
                ea = [jnp.exp(v - sv1[u][0]) for v in sv1[u]]
                eb = [jnp.exp(v - sv2[u][0]) for v in sv2[u]]
                cnt = [jnp.zeros((heads, LANES), F32) for _ in range(topk)]
                z = jnp.zeros((heads, LANES), F32)
                for (a, b), cv in zip(pairs, cand[u]):
                    sel = cv >= tau[u]
                    cnt[a] = cnt[a] + jnp.where(sel, 1.0, 0.0)
                    z = z + jnp.where(sel, ea[a] * eb[b], 0.0)
                for a in range(topk):
                    c_ref[tt, a] = cnt[a]
                iz_ref[tt] = 1.0 / z
            return carry

        lax.fori_loop(0, nt // n_par, select, 0)

        def twice(v):
            bits = lax.bitcast_convert_type(v.astype(BF16).astype(F32), jnp.uint32)
            return bits | (bits >> 16)

        def tables(tt, carry):
            for h in range(heads):
                s1 = st_ref[2 * h, tt]
                s2 = st_ref[2 * h + 1, tt]
                r1 = rank1_ref[h, tt].reshape(nk // BF16_ROWS, BF16_ROWS, LANES)
                cnt = jnp.zeros(r1.shape, BF16)
                for a in range(topk):
                    c_a = pltpu.bitcast(jnp.broadcast_to(twice(c_ref[tt, a, h:h + 1, :]),
                                                         (BF16_ROWS // 2, LANES)), BF16)
                    cnt = jnp.where(r1 == float(a), c_a[None], cnt)
                cnt = cnt.astype(F32).reshape(nk, LANES)
                m1 = sv_ref[tt, 0, 0, h:h + 1, :]
                m2 = sv_ref[tt, 1, 0, h:h + 1, :]
                ea1 = jnp.exp(s1 - m1) * (0.5 * iz_ref[tt, h:h + 1, :])
                ta_ref[tt, pl.ds(h, nk, stride=2 * heads), :] = twice(cnt)
                ta_ref[tt, pl.ds(heads + h, nk, stride=2 * heads), :] = twice(ea1)
                eb2 = jnp.exp(s2 - m2).astype(BF16)
                for jb in range(nk // BF16_ROWS):
                    tb_ref[tt, jb, heads + h] = eb2[jb * BF16_ROWS:(jb + 1) * BF16_ROWS]
            return carry

        lax.fori_loop(0, nt, tables, 0)

    i0 = e * (et // nk)
    zero = jnp.zeros((), BF16)
    n_sub = et // PEER_SUB
    n_il = et // nk
    n_late = n_sub // 2
    early = list(range((n_il * 5) // 8))
    late = list(range(len(early), n_il))
    assert n_late >= 1 and len(early) * nk >= n_late * PEER_SUB
    xnt = xnt_ref[...]

    def bcast_row(tt, i, k):
        word_row = ta_ref[tt, pl.ds(i * (2 * heads) + k, 1), :]
        return pltpu.bitcast(jnp.broadcast_to(word_row, (BF16_ROWS // 2, LANES)), BF16)

    def hidden(k):
        sub = slice(k * PEER_SUB, (k + 1) * PEER_SUB)
        hid_ref[sub, :] = jnp.dot(u_ref[sub, :], xnt, preferred_element_type=F32)

    def build(ils, tt):
        lanes = slice(tt * LANES, (tt + 1) * LANES)
        for jb in range(nk // BF16_ROWS):
            r2 = [tb_ref[tt, jb, h] for h in range(heads)]
            b2 = [tb_ref[tt, jb, heads + h] for h in range(heads)]
            ws = {}
            for h in range(heads):
                for il in ils:
                    term = jnp.where(bcast_row(tt, i0 + il, h) > r2[h],
                                     bcast_row(tt, i0 + il, heads + h) * b2[h], zero)
                    ws[il] = term if h == 0 else ws[il] + term
            for il in ils:
                w_ref[il * nk + jb * BF16_ROWS:il * nk + (jb + 1) * BF16_ROWS, lanes] = ws[il]

    def gated(k):
        sub = slice(k * PEER_SUB, (k + 1) * PEER_SUB)
        hv = hid_ref[sub, :].astype(BF16)
        pt_ref[k % 2] = w_ref[sub, :] * hv * (1.0 + lax.erf(hv * (2.0 ** -0.5)))
        return jnp.dot(vt_ref[:, sub], pt_ref[k % 2], preferred_element_type=F32)

    for k in range(n_sub):
        hidden(k)
        for tt in range(k * nt // n_sub, (k + 1) * nt // n_sub):
            build(early, tt)
    acc = acc_ref[...]
    for k in range(n_sub):
        acc = acc + gated(k)
        if k < n_late:
            for tt in range(k * nt // n_late, (k + 1) * nt // n_late):
                build(late, tt)
    acc_ref[...] = acc

    @pl.when(e == n_e - 1)
    def _finish():
        out = x_ref[...] + acc_ref[...].T
        if final:
            out = _rmsnorm_rows(out, fg_ref[...])
        o_ref[...] = out


def _peer(xf, g, wqt, sk, u, vt, final_g=None, *, tb, et):
    t, d = xf.shape
    ne = u.shape[0]
    n_groups, nk, _ = sk.shape
    heads = n_groups // 2
    nt = tb // LANES
    final = final_g is not None
    tok = pl.BlockSpec((tb, d), lambda ti, e: (ti, 0))
    in_specs = [tok, _const_spec((1, d)), _const_spec(wqt.shape), _const_spec(sk.shape),
                pl.BlockSpec((et, d), lambda ti, e: (e, 0)),
                pl.BlockSpec((None, d, et), lambda ti, e: (e, 0, 0))]
    args = [xf, g, wqt, sk, u, vt]
    if final:
        in_specs.append(_const_spec((1, d)))
        args.append(final_g)
    scratch = [pltpu.VMEM((d, tb), BF16),
               pltpu.VMEM((n_groups, nt, nk, LANES), F32),
               pltpu.VMEM((heads, nt, nk, LANES), BF16),
               pltpu.VMEM((nt, 2, PEER_TOPK, heads, LANES), F32),
               pltpu.VMEM((nt, PEER_TOPK, heads, LANES), F32),
               pltpu.VMEM((nt, heads, LANES), F32),
               pltpu.VMEM((nt, nk * 2 * heads, LANES), jnp.uint32),
               pltpu.VMEM((nt, nk // BF16_ROWS, 2 * heads, BF16_ROWS, LANES), BF16),
               pltpu.VMEM((et, tb), F32),
               pltpu.VMEM((et, tb), BF16),
               pltpu.VMEM((2, PEER_SUB, tb), BF16),
               pltpu.VMEM((d, tb), F32)]
    return pl.pallas_call(
        functools.partial(_peer_body, heads=heads, topk=PEER_TOPK, final=final),
        grid=(t // tb, ne // et),
        in_specs=in_specs,
        out_specs=tok,
        out_shape=jax.ShapeDtypeStruct((t, d), F32),
        scratch_shapes=scratch,
        compiler_params=_params("parallel", "arbitrary"),
        name="peer_final" if final else "peer",
    )(*args)


SEQ_TILE = 512
PEER_TOKENS = 512
PEER_EXPERTS = 1024
PEER_SUB = 256


def _rope_tables(seq_len, half):
    freqs = 1.0 / (ROPE_BASE ** (jnp.arange(half, dtype=F32) / half))
    ang = jnp.arange(seq_len, dtype=F32)[:, None] * freqs[None, :]
    return jnp.cos(ang), jnp.sin(ang)


def _trunk(x, p):
    b, l, d = x.shape
    ts = min(SEQ_TILE, l)
    tb = min(PEER_TOKENS, b * l)
    et = p["vt"].shape[-1]
    assert l % ts == 0 and ts % CHUNK == 0 and ts % CONV_ROWS == 0 and (b * l) % tb == 0 and tb % LANES == 0
    cos, sin = _rope_tables(l, d // RET_HEADS // 2)
    depth = p["w_in"].shape[0]
    for li in range(depth):
        row = lambda a: a[li][None, :]
        u, q, k, v, gs = _inproj(x.reshape(b * l, d), row(p["norm_mix_g"]), p["w_in"][li], cos, sin,
                                 seq_len=l, ts=ts)
        u, q, k, v, gs = (a.reshape(b, l, d) for a in (u, q, k, v, gs))
        lgf, lgb = p["lgf"][li], p["lgb"][li]
        cb = _retention(lgf, lgb, q, k, v, ts=ts)
        r = _retention(lgf, lgb, q, k, v, cb, gs, ts=ts)
        x = _post(x, u, r, row(p["norm_mix_g"]), p["conv_w"][li], row(p["conv_b"]), row(p["conv_ln_g"]),
                  row(p["conv_ln_b"]), p["w_conv_out"][li], p["w_ret_out"][li], p["w_gate"][li],
                  p["w_o"][li], ts=ts)
        fg = p["final_norm_g"][None, :] if li == depth - 1 else None
        x = _peer(x.reshape(b * l, d), row(p["norm_ffn_g"]), p["wqt"][li], p["sk"][li], p["u"][li],
                  p["vt"][li], fg, tb=tb, et=et).reshape(b, l, d)
    return x


def kernel(x_prompt, x_sample, norm_mix_g, w_in, w_gate, conv_w, conv_b, conv_ln_g, conv_ln_b, w_conv_out, log_gamma_fwd, log_gamma_bwd, w_ret_out, w_o, norm_ffn_g, peer_wq, peer_subkeys, peer_u, peer_v, final_norm_g):
    d = x_prompt.shape[-1]
    depth, n_heads, _, nk, dq = peer_subkeys.shape
    ne = peer_u.shape[1]
    et = min(PEER_EXPERTS, ne)
    assert w_in.shape[-1] == 6 * d and conv_w.shape[1] == CONV_KERNEL
    p = dict(
        norm_mix_g=norm_mix_g, norm_ffn_g=norm_ffn_g, final_norm_g=final_norm_g,
        conv_w=conv_w, conv_b=conv_b, conv_ln_g=conv_ln_g, conv_ln_b=conv_ln_b,
        lgf=log_gamma_fwd.astype(F32), lgb=log_gamma_bwd.astype(F32),
        w_in=w_in.astype(BF16), w_gate=w_gate.astype(BF16), w_conv_out=w_conv_out.astype(BF16),
        w_ret_out=w_ret_out.astype(BF16), w_o=w_o.astype(BF16),
        wqt=jnp.swapaxes(peer_wq, 1, 2).astype(BF16),
        sk=peer_subkeys.reshape(depth, n_heads * 2, nk, dq).astype(BF16),
        u=peer_u.astype(BF16),
        vt=jnp.swapaxes(peer_v.reshape(depth, ne // et, et, d), 2, 3).astype(BF16),
    )
    return _trunk(x_prompt, p), _trunk(x_sample, p)
```

```python
import functools
import math

import jax
import jax.numpy as jnp
from jax import lax
from jax.experimental import pallas as pl
from jax.experimental.pallas import tpu as pltpu

F32 = jnp.float32
BF16 = jnp.bfloat16

EPS = 1e-6
RET_HEADS = 4
CHUNK = 128
ROPE_BASE = 10000.0
CONV_KERNEL = 31
PEER_TOPK = 16

LANES = 128
SUBLANES = 8
BF16_ROWS = 16
CONV_HALO = 16
VMEM_LIMIT_BYTES = 56 * 1024 * 1024


def _const_spec(shape):
    nd = len(shape)
    return pl.BlockSpec(shape, lambda *_: (0,) * nd, pipeline_mode=pl.Buffered(1))


def _params(*sem):
    return pltpu.CompilerParams(dimension_semantics=sem, vmem_limit_bytes=VMEM_LIMIT_BYTES)


def _rmsnorm_rows(x, g):
    return x * lax.rsqrt(jnp.mean(x * x, axis=-1, keepdims=True) + EPS) * g


def _inproj_body(x_ref, g_ref, w_ref, cos_ref, sin_ref, u_ref, q_ref, k_ref, v_ref, gs_ref, *, heads):
    x = x_ref[...]
    d = x.shape[-1]
    xn = _rmsnorm_rows(x, g_ref[...]).astype(BF16)

    def seg(n):
        return jnp.dot(xn, w_ref[:, n * d:(n + 1) * d], preferred_element_type=F32)

    u_ref[...] = (seg(0) * jax.nn.sigmoid(seg(1))).astype(BF16)
    cos = cos_ref[...]
    sin = sin_ref[...]
    dk = d // heads
    half = dk // 2

    def rope_store(z, o_ref, scale):
        for h in range(heads):
            lo = slice(h * dk, h * dk + half)
            hi = slice(h * dk + half, (h + 1) * dk)
            t1 = z[:, lo]
            t2 = z[:, hi]
            o_ref[:, lo] = ((t1 * cos - t2 * sin) * scale).astype(BF16)
            o_ref[:, hi] = ((t1 * sin + t2 * cos) * scale).astype(BF16)

    rope_store(seg(2), q_ref, 1.0)
    rope_store(seg(3), k_ref, float(dk) ** -0.5)
    v_ref[...] = seg(4).astype(BF16)
    gs_ref[...] = jax.nn.silu(seg(5)).astype(BF16)


def _inproj(xf, g, w_in, cos, sin, *, seq_len, ts):
    t, d = xf.shape
    half = cos.shape[-1]
    n_seq_tiles = seq_len // ts
    tok = pl.BlockSpec((ts, d), lambda i: (i, 0))
    pos = pl.BlockSpec((ts, half), lambda i: (i % n_seq_tiles, 0))
    out = jax.ShapeDtypeStruct((t, d), BF16)
    return pl.pallas_call(
        functools.partial(_inproj_body, heads=RET_HEADS),
        grid=(t // ts,),
        in_specs=[tok, _const_spec((1, d)), _const_spec(w_in.shape), pos, pos],
        out_specs=[tok] * 5,
        out_shape=[out] * 5,
        compiler_params=_params("parallel"),
        name="inproj",
    )(xf, g, w_in, cos, sin)


def _retention_body(lgf_ref, lgb_ref, q_ref, k_ref, v_ref, *rest, heads, chunk, final):
    if final:
        cb_ref, gs_ref, o_ref, s_ref, qd_ref, kd_ref, cd_ref, dm_ref = rest
    else:
        o_ref, s_ref, qd_ref, kd_ref, cd_ref = rest
        dm_ref = None
    d = q_ref.shape[-1]
    dk = d // heads
    c = chunk
    n_chunks = q_ref.shape[1] // c

    @pl.when(pl.program_id(1) == 0)
    def _():
        s_ref[...] = jnp.zeros(s_ref.shape, F32)
        row = lax.broadcasted_iota(jnp.int32, (c, dk), 0).astype(F32)
        ri = lax.broadcasted_iota(jnp.int32, (c, c), 0)
        ci = lax.broadcasted_iota(jnp.int32, (c, c), 1)
        for h in range(heads):
            cols = slice(h * dk, (h + 1) * dk)
            if final:
                lg = lgf_ref[h]
                qd_ref[:, cols] = jnp.exp((row + 1.0) * lg)
                kd_ref[:, cols] = jnp.exp((c - 1.0 - row) * lg)
                lower = ri >= ci
                diff = jnp.where(lower, ri - ci, ci - ri).astype(F32)
                dm_ref[h] = jnp.exp(diff * jnp.where(lower, lg, lgb_ref[h]))
            else:
                lg = lgb_ref[h]
                qd_ref[:, cols] = jnp.exp((float(c) - row) * lg)
                kd_ref[:, cols] = jnp.exp(row * lg)
            cd_ref[:, cols] = jnp.exp(jnp.full((1, dk), float(c), F32) * lg)

    order = range(n_chunks) if final else range(n_chunks - 1, -1, -1)
    for n in order:
        rows = slice(n * c, (n + 1) * c)
        for h in range(heads):
            cols = slice(h * dk, (h + 1) * dk)
            qc = q_ref[0, rows, cols]
            kc = k_ref[0, rows, cols]
            vc = v_ref[0, rows, cols]
            state = s_ref[h]
            qdec = (qc.astype(F32) * qd_ref[:, cols]).astype(BF16)
            kdec = (kc.astype(F32) * kd_ref[:, cols]).astype(BF16)
            cross = jnp.dot(qdec, state.astype(BF16), preferred_element_type=F32)
            s_ref[h] = state * cd_ref[:, cols] + lax.dot_general(
                kdec, vc, (((0,), (0,)), ((), ())), preferred_element_type=F32)
            if final:
                scores = lax.dot_general(qc, kc, (((1,), (1,)), ((), ())), preferred_element_type=F32)
                intra = jnp.dot((scores * dm_ref[h]).astype(BF16), vc, preferred_element_type=F32)
                o = intra + cross + cb_ref[0, rows, cols].astype(F32)
                o = o * lax.rsqrt(jnp.mean(o * o, axis=-1, keepdims=True) + EPS)
                o_ref[0, rows, cols] = (gs_ref[0, rows, cols].astype(F32) * o).astype(BF16)
            else:
                o_ref[0, rows, cols] = cross.astype(BF16)


def _retention(lgf, lgb, q, k, v, cb=None, gs=None, *, ts):
    b, l, d = q.shape
    final = cb is not None
    nb = l // ts
    dk = d // RET_HEADS
    if final:
        blk = pl.BlockSpec((1, ts, d), lambda bi, j: (bi, j, 0))
    else:
        blk = pl.BlockSpec((1, ts, d), lambda bi, j: (bi, nb - 1 - j, 0))
    smem = pl.BlockSpec(memory_space=pltpu.SMEM)
    scratch = [pltpu.VMEM((RET_HEADS, dk, dk), F32),
               pltpu.VMEM((CHUNK, d), F32),
               pltpu.VMEM((CHUNK, d), F32),
               pltpu.VMEM((1, d), F32)]
    args = [lgf, lgb, q, k, v]
    in_specs = [smem, smem, blk, blk, blk]
    if final:
        scratch.append(pltpu.VMEM((RET_HEADS, CHUNK, CHUNK), F32))
        args += [cb, gs]
        in_specs += [blk, blk]
    return pl.pallas_call(
        functools.partial(_retention_body, heads=RET_HEADS, chunk=CHUNK, final=final),
        grid=(b, nb),
        in_specs=in_specs,
        out_specs=blk,
        out_shape=jax.ShapeDtypeStruct((b, l, d), BF16),
        scratch_shapes=scratch,
        compiler_params=_params("arbitrary", "arbitrary"),
        name="retention_fwd" if final else "retention_bwd",
    )(*args)


CONV_ROWS = 128
CONV_COLS = 128


def _post_body(x_ref, up_ref, uc_ref, un_ref, r_ref, gmix_ref, cw_ref, cbias_ref, lng_ref, lnb_ref,
               wco_ref, wro_ref, wg_ref, wo_ref, o_ref, win_ref, y_ref, *, n_tiles):
    i = pl.program_id(1)
    ts, d = uc_ref.shape[1], uc_ref.shape[2]
    halo = up_ref.shape[1]
    n_win = ts // CONV_ROWS
    taps = cw_ref.shape[0]
    pad = taps // 2

    prev = jnp.where(i > 0, up_ref[0].astype(F32), 0.0)
    nxt = jnp.where(i < n_tiles - 1, un_ref[0].astype(F32), 0.0)
    for w in range(n_win):
        lo = w * CONV_ROWS
        win_ref[w, halo:halo + CONV_ROWS, :] = uc_ref[0, lo:lo + CONV_ROWS, :].astype(F32)
        if w == 0:
            win_ref[w, 0:halo, :] = prev
        else:
            win_ref[w, 0:halo, :] = uc_ref[0, lo - halo:lo, :].astype(F32)
        if w == n_win - 1:
            win_ref[w, halo + CONV_ROWS:, :] = nxt
        else:
            win_ref[w, halo + CONV_ROWS:, :] = uc_ref[0, lo + CONV_ROWS:lo + CONV_ROWS + halo, :].astype(F32)

    def conv_rows(w, carry):
        for cb in range(d // CONV_COLS):
            cols = slice(cb * CONV_COLS, (cb + 1) * CONV_COLS)
            window = win_ref[w, :, cols]
            n_rows = CONV_ROWS + 2 * halo
            phase = [window] + [pltpu.roll(window, n_rows - s, 0) for s in range(1, SUBLANES)]
            acc = jnp.broadcast_to(cbias_ref[:, cols], (CONV_ROWS, CONV_COLS))
            for t in range(taps):
                q, s = divmod(halo - pad + t, SUBLANES)
                acc = acc + cw_ref[t:t + 1, cols] * phase[s][q * SUBLANES:q * SUBLANES + CONV_ROWS]
            y_ref[pl.ds(pl.multiple_of(w * CONV_ROWS, CONV_ROWS), CONV_ROWS), cols] = acc
        return carry

    lax.fori_loop(0, n_win, conv_rows, 0)

    y = y_ref[...]
    mu = jnp.mean(y, axis=-1, keepdims=True)
    yc = y - mu
    var = jnp.mean(yc * yc, axis=-1, keepdims=True)
    yn = yc * lax.rsqrt(var + EPS) * lng_ref[...] + lnb_ref[...]
    y_conv = jnp.dot(jax.nn.silu(yn).astype(BF16), wco_ref[...], preferred_element_type=F32)
    y_ret = jnp.dot(r_ref[0], wro_ref[...], preferred_element_type=F32)
    x = x_ref[0]
    xn = _rmsnorm_rows(x, gmix_ref[...]).astype(BF16)
    gates = jax.nn.sigmoid(jnp.dot(xn, wg_ref[...], preferred_element_type=F32))
    merged = gates[:, :d] * y_conv + gates[:, d:] * y_ret
    o_ref[0] = x + jnp.dot(merged.astype(BF16), wo_ref[...], preferred_element_type=F32)


def _post(x, u, r, gmix, conv_w, conv_b, ln_g, ln_b, w_conv_out, w_ret_out, w_gate, w_o, *, ts):
    b, l, d = x.shape
    n_tiles = l // ts
    hpt = ts // CONV_HALO
    n_halo = l // CONV_HALO
    tile = pl.BlockSpec((1, ts, d), lambda bi, i: (bi, i, 0))
    prev = pl.BlockSpec((1, CONV_HALO, d), lambda bi, i: (bi, jnp.maximum(i * hpt - 1, 0), 0))
    nxt = pl.BlockSpec((1, CONV_HALO, d), lambda bi, i: (bi, jnp.minimum((i + 1) * hpt, n_halo - 1), 0))
    vec = _const_spec((1, d))
    return pl.pallas_call(
        functools.partial(_post_body, n_tiles=n_tiles),
        grid=(b, n_tiles),
        in_specs=[tile, prev, tile, nxt, tile, vec, _const_spec(conv_w.shape), vec, vec, vec,
                  _const_spec(w_conv_out.shape), _const_spec(w_ret_out.shape),
                  _const_spec(w_gate.shape), _const_spec(w_o.shape)],
        out_specs=tile,
        out_shape=jax.ShapeDtypeStruct((b, l, d), F32),
        scratch_shapes=[pltpu.VMEM((ts // CONV_ROWS, CONV_ROWS + 2 * CONV_HALO, d), F32),
                        pltpu.VMEM((ts, d), F32)],
        compiler_params=_params("parallel", "parallel"),
        name="post",
    )(x, u, u, u, r, gmix, conv_w, conv_b, ln_g, ln_b, w_conv_out, w_ret_out, w_gate, w_o)


def _peer_body(x_ref, g_ref, wqt_ref, sk_ref, u_ref, vt_ref, *rest, heads, topk, final):
    if final:
        fg_ref = rest[0]
        rest = rest[1:]
    (o_ref, xnt_ref, st_ref, rank1_ref, sv_ref, c_ref, iz_ref, ta_ref, tb_ref,
     hid_ref, w_ref, pt_ref, acc_ref) = rest
    e = pl.program_id(1)
    n_e = pl.num_programs(1)
    tb, d = x_ref.shape
    n_groups, nk, dq = sk_ref.shape
    nt = tb // LANES
    et = u_ref.shape[0]
    neg_inf = float("-inf")

    @pl.when(e == 0)
    def _routing():
        xn = _rmsnorm_rows(x_ref[...], g_ref[...])
        xnt = xn.T.astype(BF16)
        xnt_ref[...] = xnt
        qt = jnp.dot(wqt_ref[...], xnt, preferred_element_type=F32).astype(BF16)
        for g in range(n_groups):
            s = jnp.dot(sk_ref[g], qt[g * dq:(g + 1) * dq, :], preferred_element_type=F32)
            for tt in range(nt):
                st_ref[g, tt] = s[:, tt * LANES:(tt + 1) * LANES]
        acc_ref[...] = jnp.zeros(acc_ref.shape, F32)

        def extract(n, carry):
            h = n // nt
            tt = n % nt
            s = [st_ref[2 * h + p, tt] for p in range(2)]
            rank = [jnp.full((nk, LANES), float(topk), F32) for _ in range(2)]
            for r in range(topk):
                for p in range(2):
                    m = jnp.max(s[p], axis=0, keepdims=True)
                    eq = s[p] == m
                    rank[p] = jnp.where(eq, float(r), rank[p])
                    s[p] = jnp.where(eq, neg_inf, s[p])
                    sv_ref[tt, p, r, pl.ds(h, 1), :] = m
            rank1_ref[h, tt] = rank[0].astype(BF16)
            rank2 = rank[1].astype(BF16)
            for jb in range(nk // BF16_ROWS):
                tb_ref[tt, jb, h] = rank2[jb * BF16_ROWS:(jb + 1) * BF16_ROWS]
            return carry

        lax.fori_loop(0, heads * nt, extract, 0)

        pairs = [(a, b) for a in range(topk) for b in range(topk) if (a + 1) * (b + 1) <= topk]

        n_par = 2 if nt % 2 == 0 else 1

        def select(n, carry):
            tts = [n * n_par + u for u in range(n_par)]
            sv1 = [[sv_ref[tt, 0, r] for r in range(topk)] for tt in tts]
            sv2 = [[sv_ref[tt, 1, r] for r in range(topk)] for tt in tts]
            cand = [[sv1[u][a] + sv2[u][b] for a, b in pairs] for u in range(n_par)]
            work = [list(c) for c in cand]
            tau = [None] * n_par
            for it in range(topk):
                for u in range(n_par):
                    tau[u] = functools.reduce(jnp.maximum, work[u])
                    if it < topk - 1:
                        work[u] = [jnp.where(w == tau[u], neg_inf, w) for w in work[u]]
            for u, tt in enumerate(tts):
                ea = [jnp.exp(v - sv1[u][0]) for v in sv1[u]]
                eb = [jnp.exp(v - sv2[u][0]) for v in sv2[u]]
                cnt = [jnp.zeros((heads, LANES), F32) for _ in range(topk)]
                z = jnp.zeros((heads, LANES), F32)
                for (a, b), cv in zip(pairs, cand[u]):
                    sel = cv >= tau[u]
                    cnt[a] = cnt[a] + jnp.where(sel, 1.0, 0.0)
                    z = z + jnp.where(sel, ea[a] * eb[b], 0.0)
                for a in range(topk):
                    c_ref[tt, a] = cnt[a]
                iz_ref[tt] = 1.0 / z
            return carry

        lax.fori_loop(0, nt // n_par, select, 0)

        def twice(v):
            bits = lax.bitcast_convert_type(v.astype(BF16).astype(F32), jnp.uint32)
            return bits | (bits >> 16)

        def tables(tt, carry):
            for h in range(heads):
                s1 = st_ref[2 * h, tt]
                s2 = st_ref[2 * h + 1, tt]
                r1 = rank1_ref[h, tt].reshape(nk // BF16_ROWS, BF16_ROWS, LANES)
                cnt = jnp.zeros(r1.shape, BF16)
                for a in range(topk):
                    c_a = pltpu.bitcast(jnp.broadcast_to(twice(c_ref[tt, a, h:h + 1, :]),
                                                         (BF16_ROWS // 2, LANES)), BF16)
                    cnt = jnp.where(r1 == float(a), c_a[None], cnt)
                cnt = cnt.astype(F32).reshape(nk, LANES)
                m1 = sv_ref[tt, 0, 0, h:h + 1, :]
                m2 = sv_ref[tt, 1, 0, h:h + 1, :]
                ea1 = jnp.exp(s1 - m1) * (0.5 * iz_ref[tt, h:h + 1, :])
                ta_ref[tt, pl.ds(h, nk, stride=2 * heads), :] = twice(cnt)
                ta_ref[tt, pl.ds(heads + h, nk, stride=2 * heads), :] = twice(ea1)
                eb2 = jnp.exp(s2 - m2).astype(BF16)
                for jb in range(nk // BF16_ROWS):
                    tb_ref[tt, jb, heads + h] = eb2[jb * BF16_ROWS:(jb + 1) * BF16_ROWS]
            return carry

        lax.fori_loop(0, nt, tables, 0)

    i0 = e * (et // nk)
    zero = jnp.zeros((), BF16)
    n_sub = et // PEER_SUB
    n_il = et // nk
    n_late = n_sub // 2
    early = list(range((n_il * 5) // 8))
    late = list(range(len(early), n_il))
    assert n_late >= 1 and len(early) * nk >= n_late * PEER_SUB
    xnt = xnt_ref[...]

    def bcast_row(tt, i, k):
        word_row = ta_ref[tt, pl.ds(i * (2 * heads) + k, 1), :]
        return pltpu.bitcast(jnp.broadcast_to(word_row, (BF16_ROWS // 2, LANES)), BF16)

    def hidden(k):
        sub = slice(k * PEER_SUB, (k + 1) * PEER_SUB)
        hid_ref[sub, :] = jnp.dot(u_ref[sub, :], xnt, preferred_element_type=F32)

    def build(ils, tt):
        lanes = slice(tt * LANES, (tt + 1) * LANES)
        for jb in range(nk // BF16_ROWS):
            r2 = [tb_ref[tt, jb, h] for h in range(heads)]
            b2 = [tb_ref[tt, jb, heads + h] for h in range(heads)]
            ws = {}
            for h in range(heads):
                for il in ils:
                    term = jnp.where(bcast_row(tt, i0 + il, h) > r2[h],
                                     bcast_row(tt, i0 + il, heads + h) * b2[h], zero)
                    ws[il] = term if h == 0 else ws[il] + term
            for il in ils:
                w_ref[il * nk + jb * BF16_ROWS:il * nk + (jb + 1) * BF16_ROWS, lanes] = ws[il]

    def gated(k):
        sub = slice(k * PEER_SUB, (k + 1) * PEER_SUB)
        hv = hid_ref[sub, :].astype(BF16)
        pt_ref[k % 2] = w_ref[sub, :] * hv * (1.0 + lax.erf(hv * (2.0 ** -0.5)))
        return jnp.dot(vt_ref[:, sub], pt_ref[k % 2], preferred_element_type=F32)

    for k in range(n_sub):
        hidden(k)
        for tt in range(k * nt // n_sub, (k + 1) * nt // n_sub):
            build(early, tt)
    acc = acc_ref[...]
    for k in range(n_sub):
        acc = acc + gated(k)
        if k < n_late:
            for tt in range(k * nt // n_late, (k + 1) * nt // n_late):
                build(late, tt)
    acc_ref[...] = acc

    @pl.when(e == n_e - 1)
    def _finish():
        out = x_ref[...] + acc_ref[...].T
        if final:
            out = _rmsnorm_rows(out, fg_ref[...])
        o_ref[...] = out


def _peer(xf, g, wqt, sk, u, vt, final_g=None, *, tb, et):
    t, d = xf.shape
    ne = u.shape[0]
    n_groups, nk, _ = sk.shape
    heads = n_groups // 2
    nt = tb // LANES
    final = final_g is not None
    tok = pl.BlockSpec((tb, d), lambda ti, e: (ti, 0))
    in_specs = [tok, _const_spec((1, d)), _const_spec(wqt.shape), _const_spec(sk.shape),
                pl.BlockSpec((et, d), lambda ti, e: (e, 0)),
                pl.BlockSpec((None, d, et), lambda ti, e: (e, 0, 0))]
    args = [xf, g, wqt, sk, u, vt]
    if final:
        in_specs.append(_const_spec((1, d)))
        args.append(final_g)
    scratch = [pltpu.VMEM((d, tb), BF16),
               pltpu.VMEM((n_groups, nt, nk, LANES), F32),
               pltpu.VMEM((heads, nt, nk, LANES), BF16),
               pltpu.VMEM((nt, 2, PEER_TOPK, heads, LANES), F32),
               pltpu.VMEM((nt, PEER_TOPK, heads, LANES), F32),
               pltpu.VMEM((nt, heads, LANES), F32),
               pltpu.VMEM((nt, nk * 2 * heads, LANES), jnp.uint32),
               pltpu.VMEM((nt, nk // BF16_ROWS, 2 * heads, BF16_ROWS, LANES), BF16),
               pltpu.VMEM((et, tb), F32),
               pltpu.VMEM((et, tb), BF16),
               pltpu.VMEM((2, PEER_SUB, tb), BF16),
               pltpu.VMEM((d, tb), F32)]
    return pl.pallas_call(
        functools.partial(_peer_body, heads=heads, topk=PEER_TOPK, final=final),
        grid=(t // tb, ne // et),
        in_specs=in_specs,
        out_specs=tok,
        out_shape=jax.ShapeDtypeStruct((t, d), F32),
        scratch_shapes=scratch,
        compiler_params=_params("parallel", "arbitrary"),
        name="peer_final" if final else "peer",
    )(*args)


SEQ_TILE = 512
PEER_TOKENS = 512
PEER_EXPERTS = 2048
PEER_SUB = 256


def _rope_tables(seq_len, half):
    freqs = 1.0 / (ROPE_BASE ** (jnp.arange(half, dtype=F32) / half))
    ang = jnp.arange(seq_len, dtype=F32)[:, None] * freqs[None, :]
    return jnp.cos(ang), jnp.sin(ang)


def _trunk(x, p):
    b, l, d = x.shape
    ts = min(SEQ_TILE, l)
    tb = min(PEER_TOKENS, b * l)
    et = p["vt"].shape[-1]
    assert l % ts == 0 and ts % CHUNK == 0 and ts % CONV_ROWS == 0 and (b * l) % tb == 0 and tb % LANES == 0
    cos, sin = _rope_tables(l, d // RET_HEADS // 2)
    depth = p["w_in"].shape[0]
    for li in range(depth):
        row = lambda a: a[li][None, :]
        u, q, k, v, gs = _inproj(x.reshape(b * l, d), row(p["norm_mix_g"]), p["w_in"][li], cos, sin,
                                 seq_len=l, ts=ts)
        u, q, k, v, gs = (a.reshape(b, l, d) for a in (u, q, k, v, gs))
        lgf, lgb = p["lgf"][li], p["lgb"][li]
        cb = _retention(lgf, lgb, q, k, v, ts=ts)
        r = _retention(lgf, lgb, q, k, v, cb, gs, ts=ts)
        x = _post(x, u, r, row(p["norm_mix_g"]), p["conv_w"][li], row(p["conv_b"]), row(p["conv_ln_g"]),
                  row(p["conv_ln_b"]), p["w_conv_out"][li], p["w_ret_out"][li], p["w_gate"][li],
                  p["w_o"][li], ts=ts)
        fg = p["final_norm_g"][None, :] if li == depth - 1 else None
        x = _peer(x.reshape(b * l, d), row(p["norm_ffn_g"]), p["wqt"][li], p["sk"][li], p["u"][li],
                  p["vt"][li], fg, tb=tb, et=et).reshape(b, l, d)
    return x


def kernel(x_prompt, x_sample, norm_mix_g, w_in, w_gate, conv_w, conv_b, conv_ln_g, conv_ln_b, w_conv_out, log_gamma_fwd, log_gamma_bwd, w_ret_out, w_o, norm_ffn_g, peer_wq, peer_subkeys, peer_u, peer_v, final_norm_g):
    d = x_prompt.shape[-1]
    depth, n_heads, _, nk, dq = peer_subkeys.shape
    ne = peer_u.shape[1]
    et = min(PEER_EXPERTS, ne)
    assert w_in.shape[-1] == 6 * d and conv_w.shape[1] == CONV_KERNEL
    p = dict(
        norm_mix_g=norm_mix_g, norm_ffn_g=norm_ffn_g, final_norm_g=final_norm_g,
        conv_w=conv_w, conv_b=conv_b, conv_ln_g=conv_ln_g, conv_ln_b=conv_ln_b,
        lgf=log_gamma_fwd.astype(F32), lgb=log_gamma_bwd.astype(F32),
        w_in=w_in.astype(BF16), w_gate=w_gate.astype(BF16), w_conv_out=w_conv_out.astype(BF16),
        w_ret_out=w_ret_out.astype(BF16), w_o=w_o.astype(BF16),
        wqt=jnp.swapaxes(peer_wq, 1, 2).astype(BF16),
        sk=peer_subkeys.reshape(depth, n_heads * 2, nk, dq).astype(BF16),
        u=peer_u.astype(BF16),
        vt=jnp.swapaxes(peer_v.reshape(depth, ne // et, et, d), 2, 3).astype(BF16),
    )
    return _trunk(x_prompt, p), _trunk(x_sample, p)
```

```python
import functools
import math

import jax
import jax.numpy as jnp
from jax import lax
from jax.experimental import pallas as pl
from jax.experimental.pallas import tpu as pltpu

F32 = jnp.float32
BF16 = jnp.bfloat16

EPS = 1e-6
RET_HEADS = 4
CHUNK = 128
ROPE_BASE = 10000.0
CONV_KERNEL = 31
PEER_TOPK = 16

LANES = 128
SUBLANES = 8
BF16_ROWS = 16
CONV_HALO = 16
VMEM_LIMIT_BYTES = 56 * 1024 * 1024


def _const_spec(shape):
    nd = len(shape)
    return pl.BlockSpec(shape, lambda *_: (0,) * nd, pipeline_mode=pl.Buffered(1))


def _params(*sem):
    return pltpu.CompilerParams(dimension_semantics=sem, vmem_limit_bytes=VMEM_LIMIT_BYTES)


def _rmsnorm_rows(x, g):
    return x * lax.rsqrt(jnp.mean(x * x, axis=-1, keepdims=True) + EPS) * g


def _inproj_body(x_ref, g_ref, w_ref, cos_ref, sin_ref, u_ref, q_ref, k_ref, v_ref, gs_ref, *, heads):
    x = x_ref[...]
    d = x.shape[-1]
    xn = _rmsnorm_rows(x, g_ref[...]).astype(BF16)

    def seg(n):
        return jnp.dot(xn, w_ref[:, n * d:(n + 1) * d], preferred_element_type=F32)

    u_ref[...] = (seg(0) * jax.nn.sigmoid(seg(1))).astype(BF16)
    cos = cos_ref[...]
    sin = sin_ref[...]
    dk = d // heads
    half = dk // 2

    def rope_store(z, o_ref, scale):
        for h in range(heads):
            lo = slice(h * dk, h * dk + half)
            hi = slice(h * dk + half, (h + 1) * dk)
            t1 = z[:, lo]
            t2 = z[:, hi]
            o_ref[:, lo] = ((t1 * cos - t2 * sin) * scale).astype(BF16)
            o_ref[:, hi] = ((t1 * sin + t2 * cos) * scale).astype(BF16)

    rope_store(seg(2), q_ref, 1.0)
    rope_store(seg(3), k_ref, float(dk) ** -0.5)
    v_ref[...] = seg(4).astype(BF16)
    gs_ref[...] = jax.nn.silu(seg(5)).astype(BF16)


def _inproj(xf, g, w_in, cos, sin, *, seq_len, ts):
    t, d = xf.shape
    half = cos.shape[-1]
    n_seq_tiles = seq_len // ts
    tok = pl.BlockSpec((ts, d), lambda i: (i, 0))
    pos = pl.BlockSpec((ts, half), lambda i: (i % n_seq_tiles, 0))
    out = jax.ShapeDtypeStruct((t, d), BF16)
    return pl.pallas_call(
        functools.partial(_inproj_body, heads=RET_HEADS),
        grid=(t // ts,),
        in_specs=[tok, _const_spec((1, d)), _const_spec(w_in.shape), pos, pos],
        out_specs=[tok] * 5,
        out_shape=[out] * 5,
        compiler_params=_params("parallel"),
        name="inproj",
    )(xf, g, w_in, cos, sin)


def _retention_body(lgf_ref, lgb_ref, q_ref, k_ref, v_ref, *rest, heads, chunk, final):
    if final:
        cb_ref, gs_ref, o_ref, s_ref, qd_ref, kd_ref, cd_ref, dm_ref = rest
    else:
        o_ref, s_ref, qd_ref, kd_ref, cd_ref = rest
        dm_ref = None
    d = q_ref.shape[-1]
    dk = d // heads
    c = chunk
    n_chunks = q_ref.shape[1] // c

    @pl.when(pl.program_id(1) == 0)
    def _():
        s_ref[...] = jnp.zeros(s_ref.shape, F32)
        row = lax.broadcasted_iota(jnp.int32, (c, dk), 0).astype(F32)
        ri = lax.broadcasted_iota(jnp.int32, (c, c), 0)
        ci = lax.broadcasted_iota(jnp.int32, (c, c), 1)
        for h in range(heads):
            cols = slice(h * dk, (h + 1) * dk)
            if final:
                lg = lgf_ref[h]
                qd_ref[:, cols] = jnp.exp((row + 1.0) * lg)
                kd_ref[:, cols] = jnp.exp((c - 1.0 - row) * lg)
                lower = ri >= ci
                diff = jnp.where(lower, ri - ci, ci - ri).astype(F32)
                dm_ref[h] = jnp.exp(diff * jnp.where(lower, lg, lgb_ref[h]))
            else:
                lg = lgb_ref[h]
                qd_ref[:, cols] = jnp.exp((float(c) - row) * lg)
                kd_ref[:, cols] = jnp.exp(row * lg)
            cd_ref[:, cols] = jnp.exp(jnp.full((1, dk), float(c), F32) * lg)

    order = range(n_chunks) if final else range(n_chunks - 1, -1, -1)
    for n in order:
        rows = slice(n * c, (n + 1) * c)
        for h in range(heads):
            cols = slice(h * dk, (h + 1) * dk)
            qc = q_ref[0, rows, cols]
            kc = k_ref[0, rows, cols]
            vc = v_ref[0, rows, cols]
            state = s_ref[h]
            qdec = (qc.astype(F32) * qd_ref[:, cols]).astype(BF16)
            kdec = (kc.astype(F32) * kd_ref[:, cols]).astype(BF16)
            cross = jnp.dot(qdec, state.astype(BF16), preferred_element_type=F32)
            s_ref[h] = state * cd_ref[:, cols] + lax.dot_general(
                kdec, vc, (((0,), (0,)), ((), ())), preferred_element_type=F32)
            if final:
                scores = lax.dot_general(qc, kc, (((1,), (1,)), ((), ())), preferred_element_type=F32)
                intra = jnp.dot((scores * dm_ref[h]).astype(BF16), vc, preferred_element_type=F32)
                o = intra + cross + cb_ref[0, rows, cols].astype(F32)
                o = o * lax.rsqrt(jnp.mean(o * o, axis=-1, keepdims=True) + EPS)
                o_ref[0, rows, cols] = (gs_ref[0, rows, cols].astype(F32) * o).astype(BF16)
            else:
                o_ref[0, rows, cols] = cross.astype(BF16)


def _retention(lgf, lgb, q, k, v, cb=None, gs=None, *, ts):
    b, l, d = q.shape
    final = cb is not None
    nb = l // ts
    dk = d // RET_HEADS
    if final:
        blk = pl.BlockSpec((1, ts, d), lambda bi, j: (bi, j, 0))
    else:
        blk = pl.BlockSpec((1, ts, d), lambda bi, j: (bi, nb - 1 - j, 0))
    smem = pl.BlockSpec(memory_space=pltpu.SMEM)
    scratch = [pltpu.VMEM((RET_HEADS, dk, dk), F32),
               pltpu.VMEM((CHUNK, d), F32),
               pltpu.VMEM((CHUNK, d), F32),
               pltpu.VMEM((1, d), F32)]
    args = [lgf, lgb, q, k, v]
    in_specs = [smem, smem, blk, blk, blk]
    if final:
        scratch.append(pltpu.VMEM((RET_HEADS, CHUNK, CHUNK), F32))
        args += [cb, gs]
        in_specs += [blk, blk]
    return pl.pallas_call(
        functools.partial(_retention_body, heads=RET_HEADS, chunk=CHUNK, final=final),
        grid=(b, nb),
        in_specs=in_specs,
        out_specs=blk,
        out_shape=jax.ShapeDtypeStruct((b, l, d), BF16),
        scratch_shapes=scratch,
        compiler_params=_params("arbitrary", "arbitrary"),
        name="retention_fwd" if final else "retention_bwd",
    )(*args)


CONV_ROWS = 128
CONV_COLS = 128


def _post_body(x_ref, up_ref, uc_ref, un_ref, r_ref, gmix_ref, cw_ref, cbias_ref, lng_ref, lnb_ref,
               wco_ref, wro_ref, wg_ref, wo_ref, o_ref, win_ref, y_ref, *, n_tiles):
    i = pl.program_id(1)
    ts, d = uc_ref.shape[1], uc_ref.shape[2]
    halo = up_ref.shape[1]
    n_win = ts // CONV_ROWS
    taps = cw_ref.shape[0]
    pad = taps // 2

    prev = jnp.where(i > 0, up_ref[0].astype(F32), 0.0)
    nxt = jnp.where(i < n_tiles - 1, un_ref[0].astype(F32), 0.0)
    for w in range(n_win):
        lo = w * CONV_ROWS
        win_ref[w, halo:halo + CONV_ROWS, :] = uc_ref[0, lo:lo + CONV_ROWS, :].astype(F32)
        if w == 0:
            win_ref[w, 0:halo, :] = prev
        else:
            win_ref[w, 0:halo, :] = uc_ref[0, lo - halo:lo, :].astype(F32)
        if w == n_win - 1:
            win_ref[w, halo + CONV_ROWS:, :] = nxt
        else:
            win_ref[w, halo + CONV_ROWS:, :] = uc_ref[0, lo + CONV_ROWS:lo + CONV_ROWS + halo, :].astype(F32)

    def conv_rows(w, carry):
        for cb in range(d // CONV_COLS):
            cols = slice(cb * CONV_COLS, (cb + 1) * CONV_COLS)
            window = win_ref[w, :, cols]
            n_rows = CONV_ROWS + 2 * halo
            phase = [window] + [pltpu.roll(window, n_rows - s, 0) for s in range(1, SUBLANES)]
            acc = jnp.broadcast_to(cbias_ref[:, cols], (CONV_ROWS, CONV_COLS))
            for t in range(taps):
                q, s = divmod(halo - pad + t, SUBLANES)
                acc = acc + cw_ref[t:t + 1, cols] * phase[s][q * SUBLANES:q * SUBLANES + CONV_ROWS]
            y_ref[pl.ds(pl.multiple_of(w * CONV_ROWS, CONV_ROWS), CONV_ROWS), cols] = acc
        return carry

    lax.fori_loop(0, n_win, conv_rows, 0)

    y = y_ref[...]
    mu = jnp.mean(y, axis=-1, keepdims=True)
    yc = y - mu
    var = jnp.mean(yc * yc, axis=-1, keepdims=True)
    yn = yc * lax.rsqrt(var + EPS) * lng_ref[...] + lnb_ref[...]
    y_conv = jnp.dot(jax.nn.silu(yn).astype(BF16), wco_ref[...], preferred_element_type=F32)
    y_ret = jnp.dot(r_ref[0], wro_ref[...], preferred_element_type=F32)
    x = x_ref[0]
    xn = _rmsnorm_rows(x, gmix_ref[...]).astype(BF16)
    gates = jax.nn.sigmoid(jnp.dot(xn, wg_ref[...], preferred_element_type=F32))
    merged = gates[:, :d] * y_conv + gates[:, d:] * y_ret
    o_ref[0] = x + jnp.dot(merged.astype(BF16), wo_ref[...], preferred_element_type=F32)


def _post(x, u, r, gmix, conv_w, conv_b, ln_g, ln_b, w_conv_out, w_ret_out, w_gate, w_o, *, ts):
    b, l, d = x.shape
    n_tiles = l // ts
    hpt = ts // CONV_HALO
    n_halo = l // CONV_HALO
    tile = pl.BlockSpec((1, ts, d), lambda bi, i: (bi, i, 0))
    prev = pl.BlockSpec((1, CONV_HALO, d), lambda bi, i: (bi, jnp.maximum(i * hpt - 1, 0), 0))
    nxt = pl.BlockSpec((1, CONV_HALO, d), lambda bi, i: (bi, jnp.minimum((i + 1) * hpt, n_halo - 1), 0))
    vec = _const_spec((1, d))
    return pl.pallas_call(
        functools.partial(_post_body, n_tiles=n_tiles),
        grid=(b, n_tiles),
        in_specs=[tile, prev, tile, nxt, tile, vec, _const_spec(conv_w.shape), vec, vec, vec,
                  _const_spec(w_conv_out.shape), _const_spec(w_ret_out.shape),
                  _const_spec(w_gate.shape), _const_spec(w_o.shape)],
        out_specs=tile,
        out_shape=jax.ShapeDtypeStruct((b, l, d), F32),
        scratch_shapes=[pltpu.VMEM((ts // CONV_ROWS, CONV_ROWS + 2 * CONV_HALO, d), F32),
                        pltpu.VMEM((ts, d), F32)],
        compiler_params=_params("parallel", "parallel"),
        name="post",
    )(x, u, u, u, r, gmix, conv_w, conv_b, ln_g, ln_b, w_conv_out, w_ret_out, w_gate, w_o)


def _peer_body(x_ref, g_ref, wqt_ref, sk_ref, u_ref, vt_ref, *rest, heads, topk, final):
    if final:
        fg_ref = rest[0]
        rest = rest[1:]
    (o_ref, xnt_ref, st_ref, rank1_ref, sv_ref, c_ref, iz_ref, ta_ref, tb_ref,
     hid_ref, w_ref, pt_ref, acc_ref) = rest
    e = pl.program_id(1)
    n_e = pl.num_programs(1)
    tb, d = x_ref.shape
    n_groups, nk, dq = sk_ref.shape
    nt = tb // LANES
    et = u_ref.shape[0]
    neg_inf = float("-inf")

    @pl.when(e == 0)
    def _routing():
        xn = _rmsnorm_rows(x_ref[...], g_ref[...])
        xnt = xn.T.astype(BF16)
        xnt_ref[...] = xnt
        qt = jnp.dot(wqt_ref[...], xnt, preferred_element_type=F32).astype(BF16)
        for g in range(n_groups):
            s = jnp.dot(sk_ref[g], qt[g * dq:(g + 1) * dq, :], preferred_element_type=F32)
            for tt in range(nt):
                st_ref[g, tt] = s[:, tt * LANES:(tt + 1) * LANES]
        acc_ref[...] = jnp.zeros(acc_ref.shape, F32)

        def extract(n, carry):
            h = n // nt
            tt = n % nt
            s = [st_ref[2 * h + p, tt] for p in range(2)]
            rank = [jnp.full((nk, LANES), float(topk), F32) for _ in range(2)]
            for r in range(topk):
                for p in range(2):
                    m = jnp.max(s[p], axis=0, keepdims=True)
                    eq = s[p] == m
                    rank[p] = jnp.where(eq, float(r), rank[p])
                    s[p] = jnp.where(eq, neg_inf, s[p])
                    sv_ref[tt, p, r, pl.ds(h, 1), :] = m
            rank1_ref[h, tt] = rank[0].astype(BF16)
            rank2 = rank[1].astype(BF16)
            for jb in range(nk // BF16_ROWS):
                tb_ref[tt, jb, h] = rank2[jb * BF16_ROWS:(jb + 1) * BF16_ROWS]
            return carry

        lax.fori_loop(0, heads * nt, extract, 0)

        pairs = [(a, b) for a in range(topk) for b in range(topk) if (a + 1) * (b + 1) <= topk]

        n_par = 2 if nt % 2 == 0 else 1

        def select(n, carry):
            tts = [n * n_par + u for u in range(n_par)]
            sv1 = [[sv_ref[tt, 0, r] for r in range(topk)] for tt in tts]
            sv2 = [[sv_ref[tt, 1, r] for r in range(topk)] for tt in tts]
            cand = [[sv1[u][a] + sv2[u][b] for a, b in pairs] for u in range(n_par)]
            work = [list(c) for c in cand]
            tau = [None] * n_par
            for it in range(topk):
                for u in range(n_par):
                    tau[u] = functools.reduce(jnp.maximum, work[u])
                    if it < topk - 1:
                        work[u] = [jnp.where(w == tau[u], neg_inf, w) for w in work[u]]
            for u, tt in enumerate(tts):
                ea = [jnp.exp(v - sv1[u][0]) for v in sv1[u]]
                eb = [jnp.exp(v - sv2[u][0]) for v in sv2[u]]
                cnt = [jnp.zeros((heads, LANES), F32) for _ in range(topk)]
                z = jnp.zeros((heads, LANES), F32)
                for (a, b), cv in zip(pairs, cand[u]):
                    sel = cv >= tau[u]
                    cnt[a] = cnt[a] + jnp.where(sel, 1.0, 0.0)
                    z = z + jnp.where(sel, ea[a] * eb[b], 0.0)
                for a in range(topk):
                    c_ref[tt, a] = cnt[a]
                iz_ref[tt] = 1.0 / z
            return carry

        lax.fori_loop(0, nt // n_par, select, 0)

        def twice(v):
            bits = lax.bitcast_convert_type(v.astype(BF16).astype(F32), jnp.uint32)
            return bits | (bits >> 16)

        def tables(tt, carry):
            for h in range(heads):
                s1 = st_ref[2 * h, tt]
                s2 = st_ref[2 * h + 1, tt]
                r1 = rank1_ref[h, tt].reshape(nk // BF16_ROWS, BF16_ROWS, LANES)
                cnt = jnp.zeros(r1.shape, BF16)
                for a in range(topk):
                    c_a = pltpu.bitcast(jnp.broadcast_to(twice(c_ref[tt, a, h:h + 1, :]),
                                                         (BF16_ROWS // 2, LANES)), BF16)
                    cnt = jnp.where(r1 == float(a), c_a[None], cnt)
                cnt = cnt.astype(F32).reshape(nk, LANES)
                m1 = sv_ref[tt, 0, 0, h:h + 1, :]
                m2 = sv_ref[tt, 1, 0, h:h + 1, :]
                ea1 = jnp.exp(s1 - m1) * (0.5 * iz_ref[tt, h:h + 1, :])
                ta_ref[tt, pl.ds(h, nk, stride=2 * heads), :] = twice(cnt)
                ta_ref[tt, pl.ds(heads + h, nk, stride=2 * heads), :] = twice(ea1)
                eb2 = jnp.exp(s2 - m2).astype(BF16)
                for jb in range(nk // BF16_ROWS):
                    tb_ref[tt, jb, heads + h] = eb2[jb * BF16_ROWS:(jb + 1) * BF16_ROWS]
            return carry

        lax.fori_loop(0, nt, tables, 0)

    i0 = e * (et // nk)
    zero = jnp.zeros((), BF16)
    n_sub = et // PEER_SUB
    n_il = et // nk
    n_late = n_sub // 2
    early = list(range((n_il * 5) // 8))
    late = list(range(len(early), n_il))
    assert n_late >= 1 and len(early) * nk >= n_late * PEER_SUB
    xnt = xnt_ref[...]

    def bcast_row(tt, i, k):
        word_row = ta_ref[tt, pl.ds(i * (2 * heads) + k, 1), :]
        return pltpu.bitcast(jnp.broadcast_to(word_row, (BF16_ROWS // 2, LANES)), BF16)

    def hidden(k):
        sub = slice(k * PEER_SUB, (k + 1) * PEER_SUB)
        hid_ref[sub, :] = jnp.dot(u_ref[sub, :], xnt, preferred_element_type=F32)

    def build(ils, tt):
        lanes = slice(tt * LANES, (tt + 1) * LANES)
        for jb in range(nk // BF16_ROWS):
            r2 = [tb_ref[tt, jb, h] for h in range(heads)]
            b2 = [tb_ref[tt, jb, heads + h] for h in range(heads)]
            ws = {}
            for h in range(heads):
                for il in ils:
                    term = jnp.where(bcast_row(tt, i0 + il, h) > r2[h],
                                     bcast_row(tt, i0 + il, heads + h) * b2[h], zero)
                    ws[il] = term if h == 0 else ws[il] + term
            for il in ils:
                w_ref[il * nk + jb * BF16_ROWS:il * nk + (jb + 1) * BF16_ROWS, lanes] = ws[il]

    def gated(k):
        sub = slice(k * PEER_SUB, (k + 1) * PEER_SUB)
        hv = hid_ref[sub, :].astype(BF16)
        pt_ref[k % 2] = w_ref[sub, :] * hv * (1.0 + lax.erf(hv * (2.0 ** -0.5)))
        return jnp.dot(vt_ref[:, sub], pt_ref[k % 2], preferred_element_type=F32)

    for k in range(n_sub):
        hidden(k)
        for tt in range(k * nt // n_sub, (k + 1) * nt // n_sub):
            build(early, tt)
    acc = acc_ref[...]
    for k in range(n_sub):
        acc = acc + gated(k)
        if k < n_late:
            for tt in range(k * nt // n_late, (k + 1) * nt // n_late):
                build(late, tt)
    acc_ref[...] = acc

    @pl.when(e == n_e - 1)
    def _finish():
        out = x_ref[...] + acc_ref[...].T
        if final:
            out = _rmsnorm_rows(out, fg_ref[...])
        o_ref[...] = out


def _peer(xf, g, wqt, sk, u, vt, final_g=None, *, tb, et):
    t, d = xf.shape
    ne = u.shape[0]
    n_groups, nk, _ = sk.shape
    heads = n_groups // 2
    nt = tb // LANES
    final = final_g is not None
    tok = pl.BlockSpec((tb, d), lambda ti, e: (ti, 0))
    in_specs = [tok, _const_spec((1, d)), _const_spec(wqt.shape), _const_spec(sk.shape),
                pl.BlockSpec((et, d), lambda ti, e: (e, 0)),
                pl.BlockSpec((None, d, et), lambda ti, e: (e, 0, 0))]
    args = [xf, g, wqt, sk, u, vt]
    if final:
        in_specs.append(_const_spec((1, d)))
        args.append(final_g)
    scratch = [pltpu.VMEM((d, tb), BF16),
               pltpu.VMEM((n_groups, nt, nk, LANES), F32),
               pltpu.VMEM((heads, nt, nk, LANES), BF16),
               pltpu.VMEM((nt, 2, PEER_TOPK, heads, LANES), F32),
               pltpu.VMEM((nt, PEER_TOPK, heads, LANES), F32),
               pltpu.VMEM((nt, heads, LANES), F32),
               pltpu.VMEM((nt, nk * 2 * heads, LANES), jnp.uint32),
               pltpu.VMEM((nt, nk // BF16_ROWS, 2 * heads, BF16_ROWS, LANES), BF16),
               pltpu.VMEM((et, tb), F32),
               pltpu.VMEM((et, tb), BF16),
               pltpu.VMEM((2, PEER_SUB, tb), BF16),
               pltpu.VMEM((d, tb), F32)]
    return pl.pallas_call(
        functools.partial(_peer_body, heads=heads, topk=PEER_TOPK, final=final),
        grid=(t // tb, ne // et),
        in_specs=in_specs,
        out_specs=tok,
        out_shape=jax.ShapeDtypeStruct((t, d), F32),
        scratch_shapes=scratch,
        compiler_params=_params("parallel", "arbitrary"),
        name="peer_final" if final else "peer",
    )(*args)


SEQ_TILE = 512
RETENTION_TILE = 1024
PEER_TOKENS = 512
PEER_EXPERTS = 2048
PEER_SUB = 256


def _rope_tables(seq_len, half):
    freqs = 1.0 / (ROPE_BASE ** (jnp.arange(half, dtype=F32) / half))
    ang = jnp.arange(seq_len, dtype=F32)[:, None] * freqs[None, :]
    return jnp.cos(ang), jnp.sin(ang)


def _trunk(x, p):
    b, l, d = x.shape
    ts = min(SEQ_TILE, l)
    rts = min(RETENTION_TILE, l)
    tb = min(PEER_TOKENS, b * l)
    et = p["vt"].shape[-1]
    assert l % ts == 0 and ts % CONV_ROWS == 0 and l % rts == 0 and rts % CHUNK == 0
    assert (b * l) % tb == 0 and tb % LANES == 0
    cos, sin = _rope_tables(l, d // RET_HEADS // 2)
    depth = p["w_in"].shape[0]
    for li in range(depth):
        row = lambda a: a[li][None, :]
        u, q, k, v, gs = _inproj(x.reshape(b * l, d), row(p["norm_mix_g"]), p["w_in"][li], cos, sin,
                                 seq_len=l, ts=ts)
        u, q, k, v, gs = (a.reshape(b, l, d) for a in (u, q, k, v, gs))
        lgf, lgb = p["lgf"][li], p["lgb"][li]
        cb = _retention(lgf, lgb, q, k, v, ts=rts)
        r = _retention(lgf, lgb, q, k, v, cb, gs, ts=rts)
        x = _post(x, u, r, row(p["norm_mix_g"]), p["conv_w"][li], row(p["conv_b"]), row(p["conv_ln_g"]),
                  row(p["conv_ln_b"]), p["w_conv_out"][li], p["w_ret_out"][li], p["w_gate"][li],
                  p["w_o"][li], ts=ts)
        fg = p["final_norm_g"][None, :] if li == depth - 1 else None
        x = _peer(x.reshape(b * l, d), row(p["norm_ffn_g"]), p["wqt"][li], p["sk"][li], p["u"][li],
                  p["vt"][li], fg, tb=tb, et=et).reshape(b, l, d)
    return x


def kernel(x_prompt, x_sample, norm_mix_g, w_in, w_gate, conv_w, conv_b, conv_ln_g, conv_ln_b, w_conv_out, log_gamma_fwd, log_gamma_bwd, w_ret_out, w_o, norm_ffn_g, peer_wq, peer_subkeys, peer_u, peer_v, final_norm_g):
    d = x_prompt.shape[-1]
    depth, n_heads, _, nk, dq = peer_subkeys.shape
    ne = peer_u.shape[1]
    et = min(PEER_EXPERTS, ne)
    assert w_in.shape[-1] == 6 * d and conv_w.shape[1] == CONV_KERNEL
    p = dict(
        norm_mix_g=norm_mix_g, norm_ffn_g=norm_ffn_g, final_norm_g=final_norm_g,
        conv_w=conv_w, conv_b=conv_b, conv_ln_g=conv_ln_g, conv_ln_b=conv_ln_b,
        lgf=log_gamma_fwd.astype(F32), lgb=log_gamma_bwd.astype(F32),
        w_in=w_in.astype(BF16), w_gate=w_gate.astype(BF16), w_conv_out=w_conv_out.astype(BF16),
        w_ret_out=w_ret_out.astype(BF16), w_o=w_o.astype(BF16),
        wqt=jnp.swapaxes(peer_wq, 1, 2).astype(BF16),
        sk=peer_subkeys.reshape(depth, n_heads * 2, nk, dq).astype(BF16),
        u=peer_u.astype(BF16),
        vt=jnp.swapaxes(peer_v.reshape(depth, ne // et, et, d), 2, 3).astype(BF16),
    )
    return _trunk(x_prompt, p), _trunk(x_sample, p)
```
